```python
import math
import jax, jax.numpy as jnp
from jax import lax
import numpy as np

D_MODEL = 1024
BATCH = 1
SEQ = 16384
DEPTH = 2
DEC_BATCH = 32
DEC_SEQ = 8
PAST_LEN = 16384
PAGE_SIZE = 128

N_MIXERS = 2
N_META = 16
RMS_EPS = 1e-6
SSM_EXPAND = 2
D_INNER = SSM_EXPAND * D_MODEL
SSM_HEAD_DIM = 64
SSM_HEADS = D_INNER // SSM_HEAD_DIM
SSM_GROUPS = 4
SSM_HEADS_PER_GROUP = SSM_HEADS // SSM_GROUPS
SSM_STATE = 128
CONV_WIDTH = 4
CONV_DIM = D_INNER + 2 * SSM_GROUPS * SSM_STATE
IN_PROJ_DIM = D_INNER + CONV_DIM + SSM_HEADS
SSM_CHUNK = 128
ATTN_HEAD_DIM = 64
ATTN_HEADS = D_MODEL // ATTN_HEAD_DIM
Q_BLOCK = 128
SB_BIAS_MIN = -8.0
SB_BIAS_MAX = -4.0
N_EXPERT_GROUPS = 4
EXPERTS_PER_GROUP = 8
N_EXPERTS = N_EXPERT_GROUPS * EXPERTS_PER_GROUP
TOP_K_IN_GROUP = 2
D_EXPERT = D_MODEL // 2

kernel_name = "hybrid_ssd_stickbreak_hmoe_step"


def rmsnorm(x, w):
    x32 = x.astype(jnp.float32)
    y = x32 * lax.rsqrt(jnp.mean(x32 * x32, axis=-1, keepdims=True) + RMS_EPS)
    return (y * w.astype(jnp.float32)).astype(x.dtype)


def ssd_chunked(x, dt, A, Bm, Cm, chunk, init_state):
    out_dtype = x.dtype
    f32 = jnp.float32
    Bsz, T = x.shape[0], x.shape[1]
    pad = (-T) % chunk

    def fpad(a):
        return jnp.pad(a.astype(f32), ((0, 0), (pad, 0)) + ((0, 0),) * (a.ndim - 2))

    x, dt, Bm, Cm = fpad(x), fpad(dt), fpad(Bm), fpad(Cm)
    nc = (T + pad) // chunk
    G, R = SSM_GROUPS, SSM_HEADS_PER_GROUP
    xc = x.reshape(Bsz, nc, chunk, G, R, SSM_HEAD_DIM)
    dtc = dt.reshape(Bsz, nc, chunk, G, R)
    Bc = Bm.reshape(Bsz, nc, chunk, G, SSM_STATE)
    Cc = Cm.reshape(Bsz, nc, chunk, G, SSM_STATE)
    xdt = xc * dtc[..., None]
    acs = jnp.cumsum(dtc * A.astype(f32).reshape(G, R), axis=2)
    at = jnp.moveaxis(acs, 2, -1)
    tri = jnp.tril(jnp.ones((chunk, chunk), dtype=bool))
    lmat = jnp.exp(jnp.where(tri, at[..., :, None] - at[..., None, :], -jnp.inf))
    cb = jnp.einsum('bclgn,bcsgn->bcgls', Cc, Bc)
    y_diag = jnp.einsum('bcgrls,bcsgrp->bclgrp', cb[:, :, :, None] * lmat, xdt)
    decay = jnp.exp(acs[:, :, -1:] - acs)
    chunk_states = jnp.einsum('bclgn,bclgr,bclgrp->bcgrpn', Bc, decay, xdt)
    chunk_decay = jnp.exp(acs[:, :, -1])

    def step(s, inp):
        st, dec = inp
        return s * dec[..., None, None] + st, s

    s0 = init_state.astype(f32).reshape(Bsz, G, R, SSM_HEAD_DIM, SSM_STATE)
    final, prev = lax.scan(step, s0, (jnp.moveaxis(chunk_states, 1, 0), jnp.moveaxis(chunk_decay, 1, 0)))
    prev = jnp.moveaxis(prev, 0, 1)
    y_off = jnp.einsum('bclgn,bcgrpn,bclgr->bclgrp', Cc, prev, jnp.exp(acs))
    y = (y_diag + y_off).reshape(Bsz, nc * chunk, SSM_HEADS, SSM_HEAD_DIM)[:, pad:]
    return y.astype(out_dtype), final.reshape(Bsz, SSM_HEADS, SSM_HEAD_DIM, SSM_STATE).astype(out_dtype)


def ssd_mixer(h, conv_buf, ssm_state, chunk, w_in, conv_w, conv_b, dt_bias, A_log, D_skip, norm_w, w_out):
    Bsz, T, _ = h.shape
    zxbcdt = h @ w_in
    z = zxbcdt[..., :D_INNER]
    xbc = zxbcdt[..., D_INNER:D_INNER + CONV_DIM]
    dt_raw = zxbcdt[..., D_INNER + CONV_DIM:]
    xpad = jnp.concatenate([conv_buf.astype(xbc.dtype), xbc], axis=1)
    conv = sum(xpad[:, k:k + T] * conv_w[k] for k in range(CONV_WIDTH)) + conv_b
    new_conv = xpad[:, T:]
    xbc = jax.nn.silu(conv)
    xs = xbc[..., :D_INNER].reshape(Bsz, T, SSM_HEADS, SSM_HEAD_DIM)
    Bm = xbc[..., D_INNER:D_INNER + SSM_GROUPS * SSM_STATE].reshape(Bsz, T, SSM_GROUPS, SSM_STATE)
    Cm = xbc[..., D_INNER + SSM_GROUPS * SSM_STATE:].reshape(Bsz, T, SSM_GROUPS, SSM_STATE)
    dt = jax.nn.softplus(dt_raw.astype(jnp.float32) + dt_bias.astype(jnp.float32))
    A = -jnp.exp(A_log.astype(jnp.float32))
    y, final = ssd_chunked(xs, dt, A, Bm, Cm, chunk, ssm_state)
    y = (y + xs * D_skip[:, None]).reshape(Bsz, T, D_INNER)
    g = (y * jax.nn.silu(z)).astype(jnp.float32).reshape(Bsz, T, SSM_GROUPS, D_INNER // SSM_GROUPS)
    g = g * lax.rsqrt(jnp.mean(g * g, axis=-1, keepdims=True) + RMS_EPS)
    g = (g.reshape(Bsz, T, D_INNER) * norm_w.astype(jnp.float32)).astype(h.dtype)
    return g @ w_out, final, new_conv


def sb_attend(q, qpos, k, v, kpos, sb_bias):
    z = (jnp.einsum('bqhd,bkhd->bhqk', q, k).astype(jnp.float32) * (ATTN_HEAD_DIM ** -0.5)
         + sb_bias.astype(jnp.float32)[None, :, None, None])
    visible = kpos[None, :] < qpos[:, None]
    log_keep = jnp.where(visible, jax.nn.log_sigmoid(-z), 0.0)
    log_after = lax.cumsum(log_keep, axis=3, reverse=True) - log_keep
    attn = jnp.where(visible, jnp.exp(jax.nn.log_sigmoid(z) + log_after), 0.0)
    return jnp.einsum('bhqk,bkhd->bqhd', attn.astype(v.dtype), v)


def split_qkv(h, w_qkv):
    Bsz, T, _ = h.shape
    qkv = (h @ w_qkv).reshape(Bsz, T, 3, ATTN_HEADS, ATTN_HEAD_DIM)
    return qkv[:, :, 0], qkv[:, :, 1], qkv[:, :, 2]


def sb_prompt(h, w_qkv, w_o, sb_bias):
    Bsz, T, _ = h.shape
    q, k, v = split_qkv(h, w_qkv)
    pad = (-T) % Q_BLOCK
    nb = (T + pad) // Q_BLOCK
    qb = jnp.pad(q, ((0, 0), (0, pad), (0, 0), (0, 0))).reshape(Bsz, nb, Q_BLOCK, ATTN_HEADS, ATTN_HEAD_DIM)
    qb = jnp.swapaxes(qb, 0, 1)
    kpos = jnp.arange(T)

    def block(args):
        qi, i = args
        qpos = i * Q_BLOCK + jnp.arange(Q_BLOCK)
        return sb_attend(qi, qpos, k, v, kpos, sb_bias)

    o = lax.map(block, (qb, jnp.arange(nb)))
    o = jnp.swapaxes(o, 0, 1).reshape(Bsz, nb * Q_BLOCK, D_MODEL)[:, :T]
    return o @ w_o, k, v


def sb_sample(h, cache_k, cache_v, page_table, w_qkv, w_o, sb_bias):
    Bsz, T, _ = h.shape
    q, k, v = split_qkv(h, w_qkv)
    kpos = jnp.arange(PAST_LEN + T)
    qpos = PAST_LEN + jnp.arange(T)

    def one(args):
        qb, kb, vb, pt = args
        kp = cache_k[pt].reshape(-1, ATTN_HEADS, ATTN_HEAD_DIM).astype(kb.dtype)
        vp = cache_v[pt].reshape(-1, ATTN_HEADS, ATTN_HEAD_DIM).astype(vb.dtype)
        kk = jnp.concatenate([kp, kb], axis=0)
        vv = jnp.concatenate([vp, vb], axis=0)
        return sb_attend(qb[None], qpos, kk[None], vv[None], kpos, sb_bias)[0]

    o = lax.map(one, (q, k, v, page_table)).reshape(Bsz, T, D_MODEL)
    return o @ w_o, k, v


def hier_moe(h, w_group, w_expert, w1, w3, w2):
    Bsz, T, _ = h.shape
    f32 = jnp.float32
    g_logits = jnp.einsum('btd,dg->btg', h, w_group).astype(f32)
    g_prob = jax.nn.softmax(g_logits, axis=-1)
    g_idx = jnp.argmax(g_logits, axis=-1)
    g_sel = jax.nn.one_hot(g_idx, N_EXPERT_GROUPS, dtype=f32)
    g_w = jnp.sum(g_prob * g_sel, axis=-1, keepdims=True)
    e_logits = jnp.einsum('btd,de->bte', h, w_expert).astype(f32).reshape(Bsz, T, N_EXPERT_GROUPS, EXPERTS_PER_GROUP)
    e_in = jnp.einsum('btge,btg->bte', e_logits, g_sel)
    top_v, top_i = lax.top_k(e_in, TOP_K_IN_GROUP)
    top_w = jax.nn.softmax(top_v, axis=-1) * g_w
    eid = g_idx[..., None] * EXPERTS_PER_GROUP + top_i
    gate = jnp.sum(jax.nn.one_hot(eid, N_EXPERTS, dtype=f32) * top_w[..., None], axis=-2)
    a = jnp.einsum('btd,edf->btef', h, w1)
    u = jnp.einsum('btd,edf->btef', h, w3)
    hid = jax.nn.silu(a) * u * gate[..., None].astype(h.dtype)
    return jnp.einsum('btef,efd->btd', hid, w2)


def setup_inputs(seed: int = 0) -> dict:
    key = jax.random.key(seed)
    ks = jax.random.split(key, 32)
    f32 = jnp.float32
    n_pages = PAST_LEN // PAGE_SIZE
    n_pool = (5 * DEC_BATCH * n_pages + 3) // 4

    def nrm(k, shape, scale=1.0):
        return jax.random.normal(k, shape, f32) * scale

    dt0 = jnp.exp(jax.random.uniform(ks[14], (SSM_HEADS,), f32, minval=math.log(1e-3), maxval=math.log(1e-1)))
    return {
        "x_prompt": nrm(ks[0], (BATCH, SEQ, D_MODEL)),
        "x_sample": nrm(ks[1], (DEC_BATCH, DEC_SEQ, D_MODEL)),
        "state_ssm": nrm(ks[2], (DEC_BATCH, SSM_HEADS, SSM_HEAD_DIM, SSM_STATE), 0.1),
        "state_conv": nrm(ks[3], (DEC_BATCH, CONV_WIDTH - 1, CONV_DIM)),
        "cache_k": nrm(ks[4], (n_pool, PAGE_SIZE, ATTN_HEADS, ATTN_HEAD_DIM)),
        "cache_v": nrm(ks[5], (n_pool, PAGE_SIZE, ATTN_HEADS, ATTN_HEAD_DIM)),
        "page_table": jax.random.permutation(ks[6], n_pool)[:DEC_BATCH * n_pages].reshape(DEC_BATCH, n_pages).astype(jnp.int32),
        "meta_tokens": nrm(ks[7], (N_META, D_MODEL)),
        "norm_mix": 1.0 + nrm(ks[8], (DEPTH, D_MODEL), 0.02),
        "norm_ffn": 1.0 + nrm(ks[9], (DEPTH, D_MODEL), 0.02),
        "norm_final": 1.0 + nrm(ks[10], (D_MODEL,), 0.02),
        "m_w_in": nrm(ks[11], (D_MODEL, IN_PROJ_DIM), D_MODEL ** -0.5),
        "m_conv_w": nrm(ks[12], (CONV_WIDTH, CONV_DIM), CONV_WIDTH ** -0.5),
        "m_conv_b": nrm(ks[13], (CONV_DIM,), 0.02),
        "m_dt_bias": dt0 + jnp.log(-jnp.expm1(-dt0)),
        "m_A_log": jnp.log(jax.random.uniform(ks[15], (SSM_HEADS,), f32, minval=1.0, maxval=16.0)),
        "m_D": 1.0 + nrm(ks[16], (SSM_HEADS,), 0.02),
        "m_norm_w": 1.0 + nrm(ks[17], (D_INNER,), 0.02),
        "m_w_out": nrm(ks[18], (D_INNER, D_MODEL), D_INNER ** -0.5),
        "a_w_qkv": nrm(ks[19], (D_MODEL, 3 * ATTN_HEADS * ATTN_HEAD_DIM), D_MODEL ** -0.5),
        "a_w_o": nrm(ks[20], (ATTN_HEADS * ATTN_HEAD_DIM, D_MODEL), D_MODEL ** -0.5),
        "a_logit_bias": jax.random.uniform(ks[26], (ATTN_HEADS,), f32, minval=SB_BIAS_MIN, maxval=SB_BIAS_MAX),
        "moe_w_group": nrm(ks[21], (DEPTH, D_MODEL, N_EXPERT_GROUPS), D_MODEL ** -0.5),
        "moe_w_expert": nrm(ks[22], (DEPTH, D_MODEL, N_EXPERTS), D_MODEL ** -0.5),
        "moe_w1": nrm(ks[23], (DEPTH, N_EXPERTS, D_MODEL, D_EXPERT), D_MODEL ** -0.5),
        "moe_w3": nrm(ks[24], (DEPTH, N_EXPERTS, D_MODEL, D_EXPERT), D_MODEL ** -0.5),
        "moe_w2": nrm(ks[25], (DEPTH, N_EXPERTS, D_EXPERT, D_MODEL), D_EXPERT ** -0.5),
    }


def reference(x_prompt, x_sample, state_ssm, state_conv, cache_k, cache_v, page_table, meta_tokens,
              norm_mix, norm_ffn, norm_final, m_w_in, m_conv_w, m_conv_b, m_dt_bias, m_A_log, m_D,
              m_norm_w, m_w_out, a_w_qkv, a_w_o, a_logit_bias, moe_w_group, moe_w_expert, moe_w1, moe_w3, moe_w2):
    bp = x_prompt.shape[0]
    meta = jnp.broadcast_to(meta_tokens[None].astype(x_prompt.dtype), (bp, N_META, D_MODEL))
    hp = jnp.concatenate([meta, x_prompt], axis=1)
    hs = x_sample
    for i in range(DEPTH):
        if i % N_MIXERS == 0:
            conv0 = jnp.zeros((bp, CONV_WIDTH - 1, CONV_DIM), hp.dtype)
            ssm0 = jnp.zeros((bp, SSM_HEADS, SSM_HEAD_DIM, SSM_STATE), hp.dtype)
            mp, ssm_p, conv_p = ssd_mixer(rmsnorm(hp, norm_mix[i]), conv0, ssm0, SSM_CHUNK,
                                          m_w_in, m_conv_w, m_conv_b, m_dt_bias, m_A_log, m_D, m_norm_w, m_w_out)
            ms, ssm_s, conv_s = ssd_mixer(rmsnorm(hs, norm_mix[i]), state_conv, state_ssm, hs.shape[1],
                                          m_w_in, m_conv_w, m_conv_b, m_dt_bias, m_A_log, m_D, m_norm_w, m_w_out)
        else:
            mp, k_p, v_p = sb_prompt(rmsnorm(hp, norm_mix[i]), a_w_qkv, a_w_o, a_logit_bias)
            ms, k_s, v_s = sb_sample(rmsnorm(hs, norm_mix[i]), cache_k, cache_v, page_table, a_w_qkv, a_w_o, a_logit_bias)
        hp = hp + mp
        hs = hs + ms
        hp = hp + hier_moe(rmsnorm(hp, norm_ffn[i]), moe_w_group[i], moe_w_expert[i], moe_w1[i], moe_w3[i], moe_w2[i])
        hs = hs + hier_moe(rmsnorm(hs, norm_ffn[i]), moe_w_group[i], moe_w_expert[i], moe_w1[i], moe_w3[i], moe_w2[i])
    y_prompt = rmsnorm(hp, norm_final)[:, N_META:]
    y_sample = rmsnorm(hs, norm_final)
    return (y_prompt, y_sample, ssm_p, conv_p, k_p, v_p, ssm_s, conv_s, k_s, v_s)
```

```python
import functools
import math

import jax
import jax.numpy as jnp
from jax import lax
from jax.experimental import pallas as pl
from jax.experimental.pallas import tpu as pltpu

F32 = jnp.float32
BF16 = jnp.bfloat16
I32 = jnp.int32

RMS_EPS = 1e-6
LOG2E = 1.4426950408889634
NEG_BIG = -3.0e38

LANES = 128
SUBLANES = 8
VMEM_LIMIT = 48 * 1024 * 1024

SSD_CHUNK = 128
ROW_TILE = 512
ATTN_TQ = 256
ATTN_TK = 128
MOE_TM = 256
TOP_K = 2
PAGES_PER_STEP = 8


def _round_up(x, m):
    return (x + m - 1) // m * m


def _dot(a, b):
    return jnp.dot(a, b, preferred_element_type=F32)


def _dot_nt(a, b):
    return lax.dot_general(a, b, (((1,), (1,)), ((), ())), preferred_element_type=F32)


def _split3(x):
    hi = x.astype(BF16)
    r1 = x - hi.astype(F32)
    mid = r1.astype(BF16)
    lo = (r1 - mid.astype(F32)).astype(BF16)
    return hi, mid, lo


def _dot_exact_rhs(sel, x):
    hi, mid, lo = _split3(x)
    return _dot(sel, hi) + _dot(sel, mid) + _dot(sel, lo)


def _dot_exact_lhs(x, sel):
    hi, mid, lo = _split3(x)
    return _dot(hi, sel) + _dot(mid, sel) + _dot(lo, sel)


def _silu(x):
    return x / (1.0 + jnp.exp(-x))


def _softplus(x):
    return jnp.maximum(x, 0.0) + jnp.log1p(jnp.exp(-jnp.abs(x)))


def _linear_body(*refs, n_add, scale_lanes, has_scale, has_norm, n_w, has_res,
                 emit_x, emit_norm, n_outs, tn):
    it = iter(refs)
    add_refs = [next(it) for _ in range(n_add)]
    sc_ref = next(it) if has_scale else None
    nw_ref = next(it) if has_norm else None
    w_refs = [next(it) for _ in range(n_w)]
    res_ref = next(it) if has_res else None
    xo_ref = next(it) if emit_x else None
    no_ref = next(it) if emit_norm else None
    out_refs = [[next(it) for _ in range(k)] for k in n_outs]
    xb_ref = next(it) if n_w else None

    x = None
    for r, lane in zip(add_refs, scale_lanes):
        v = r[...]
        if lane is not None:
            v = v * sc_ref[:, lane:lane + 1]
        x = v if x is None else x + v
    if emit_x:
        xo_ref[...] = x
    if has_norm:
        ms = jnp.mean(x * x, axis=-1, keepdims=True)
        x = (x * lax.rsqrt(ms + RMS_EPS)) * nw_ref[...]
        if emit_norm:
            no_ref[...] = x
    if not n_w:
        return
    xb_ref[...] = x.astype(BF16)
    for k, w_ref in enumerate(w_refs):
        n_cols = w_ref.shape[1]
        for n0 in range(0, n_cols, tn):
            n1 = min(n0 + tn, n_cols)
            r = _dot(xb_ref[...], w_ref[:, n0:n1])
            if has_res and k == 0:
                r = r + res_ref[:, n0:n1]
            for o_ref in out_refs[k]:
                o_ref[:, n0:n1] = r.astype(o_ref.dtype)


def fused_linear(addends, *, rows, scale=None, scale_lanes=None, norm_w=None, weights=(),
                 out_dtypes=(), residual=None, emit_x=False, emit_norm=False,
                 tm=256, tn=512, name="fused_linear"):
    n_add = len(addends)
    if scale_lanes is None:
        scale_lanes = (None,) * n_add
    d_in = addends[0][0].shape[1]
    assert rows % tm == 0
    in_specs, args = [], []
    for arr, row0 in addends:
        assert row0 % tm == 0
        in_specs.append(pl.BlockSpec((tm, d_in), functools.partial(lambda i, b: (i + b, 0), b=row0 // tm)))
        args.append(arr)
    if scale is not None:
        in_specs.append(pl.BlockSpec((tm, LANES), lambda i: (i, 0)))
        args.append(scale)
    if norm_w is not None:
        in_specs.append(pl.BlockSpec((1, d_in), lambda i: (0, 0)))
        args.append(norm_w.reshape(1, d_in).astype(F32))
    for w in weights:
        in_specs.append(pl.BlockSpec(w.shape, lambda i: (0, 0)))
        args.append(w)
    if residual is not None:
        in_specs.append(pl.BlockSpec((tm, residual.shape[1]), lambda i: (i, 0)))
        args.append(residual)
    out_shapes, out_specs = [], []
    if emit_x:
        out_shapes.append(jax.ShapeDtypeStruct((rows, d_in), F32))
        out_specs.append(pl.BlockSpec((tm, d_in), lambda i: (i, 0)))
    if emit_norm:
        out_shapes.append(jax.ShapeDtypeStruct((rows, d_in), F32))
        out_specs.append(pl.BlockSpec((tm, d_in), lambda i: (i, 0)))
    for w, dts in zip(weights, out_dtypes):
        for dt in dts:
            out_shapes.append(jax.ShapeDtypeStruct((rows, w.shape[1]), dt))
            out_specs.append(pl.BlockSpec((tm, w.shape[1]), lambda i: (i, 0)))
    scratch = [pltpu.VMEM((tm, d_in), BF16)] if weights else []
    body = functools.partial(
        _linear_body, n_add=n_add, scale_lanes=tuple(scale_lanes), has_scale=scale is not None,
        has_norm=norm_w is not None, n_w=len(weights), has_res=residual is not None,
        emit_x=emit_x, emit_norm=emit_norm, n_outs=tuple(len(d) for d in out_dtypes), tn=tn)
    return pl.pallas_call(
        body,
        grid=(rows // tm,),
        in_specs=in_specs,
        out_specs=out_specs,
        out_shape=out_shapes,
        scratch_shapes=scratch,
        compiler_params=pltpu.CompilerParams(
            dimension_semantics=("arbitrary",), vmem_limit_bytes=VMEM_LIMIT),
        name=name,
    )(*args)


def _ssd_body(xbc_ref, dt_ref, z_ref, s0_ref, cw_ref, cb_ref, dtb_ref, alog_ref, dsk_ref,
              nw_ref, e_ref, et_ref, g_ref, sout_ref, ext_ref, s_ref, y_ref, *,
              n_first, d_inner, n_groups, d_state, n_heads):
    c = pl.program_id(1)
    L = SSD_CHUNK
    hd = d_inner // n_heads
    gw = n_groups * d_state
    hpg = n_heads // n_groups

    @pl.when(c == 0)
    def _():
        ext_ref[0:SUBLANES, :] = jnp.zeros((SUBLANES, ext_ref.shape[1]), F32)
        s_ref[...] = s0_ref[0]

    ext_ref[SUBLANES:SUBLANES + L, :] = xbc_ref[0]
    conv = cb_ref[...] + ext_ref[5:5 + L, :] * cw_ref[0:1, :]
    conv = conv + ext_ref[6:6 + L, :] * cw_ref[1:2, :]
    conv = conv + ext_ref[7:7 + L, :] * cw_ref[2:3, :]
    conv = conv + ext_ref[8:8 + L, :] * cw_ref[3:4, :]
    ext_ref[0:SUBLANES, :] = ext_ref[L:L + SUBLANES, :]

    row = lax.broadcasted_iota(I32, (L, 1), 0)
    n0 = jnp.where(c == 0, n_first, 0)
    valid = (row >= n0).astype(F32)
    xbc = _silu(conv) * valid
    xs = xbc[:, :d_inner]

    dt = _softplus(dt_ref[0] + dtb_ref[...]) * valid
    a = dt * (-jnp.exp(alog_ref[...]))
    ri = lax.broadcasted_iota(I32, (L, L), 0)
    ci = lax.broadcasted_iota(I32, (L, L), 1)
    tri = ri >= ci
    acs = _dot_exact_rhs(tri.astype(BF16), a)
    acs_t = acs.T
    acs_last = acs[L - 1:L, :]

    e_sel = e_ref[...]
    dt_e = _dot_exact_lhs(dt, e_sel)
    grow_e = _dot_exact_lhs(jnp.exp(acs), e_sel)
    dec_e = _dot_exact_lhs(jnp.exp(acs_last - acs), e_sel)
    xdt = xs * dt_e
    xw = xdt * dec_e

    lane = lax.broadcasted_iota(I32, (L, LANES), 1)
    lo_half = lane < hd
    s_new = []
    for g in range(n_groups):
        bg = xbc[:, d_inner + g * d_state:d_inner + (g + 1) * d_state].astype(BF16)
        cg = xbc[:, d_inner + gw + g * d_state:d_inner + gw + (g + 1) * d_state].astype(BF16)
        cb = _dot_nt(cg, bg)
        for j in range(hpg // 2):
            h0 = g * hpg + 2 * j
            ms = []
            for h in (h0, h0 + 1):
                diff = acs[:, h:h + 1] - acs_t[h:h + 1, :]
                ms.append((cb * jnp.where(tri, jnp.exp(diff), 0.0)).astype(BF16))
            lhs = jnp.concatenate(ms, axis=1)
            xp = xdt[:, h0 * hd:(h0 + 2) * hd]
            rhs = jnp.concatenate([jnp.where(lo_half, xp, 0.0), jnp.where(lo_half, 0.0, xp)],
                                  axis=0).astype(BF16)
            y_ref[:, h0 * hd:(h0 + 2) * hd] = _dot(lhs, rhs)
        c0, c1 = g * hpg * hd, (g + 1) * hpg * hd
        s_g = s_ref[c0:c1, :]
        y_off = _dot_nt(cg, s_g.astype(BF16)) * grow_e[:, c0:c1]
        y_ref[:, c0:c1] = y_ref[:, c0:c1] + y_off
        s_new.append(_dot(xw[:, c0:c1].T.astype(BF16), bg))

    last_col = jnp.broadcast_to(acs_t[:, L - 1:L], (LANES, LANES))
    cdec = jnp.exp(_dot_exact_rhs(et_ref[...], last_col))
    s_ref[...] = s_ref[...] * cdec + jnp.concatenate(s_new, axis=0)

    y = y_ref[...] + xs * dsk_ref[...]
    gt = y * _silu(z_ref[0])
    gsz = d_inner // n_groups
    outs = []
    for g in range(n_groups):
        gg = gt[:, g * gsz:(g + 1) * gsz]
        ms = jnp.mean(gg * gg, axis=-1, keepdims=True)
        outs.append(gg * lax.rsqrt(ms + RMS_EPS))
    g_ref[0] = (jnp.concatenate(outs, axis=1) * nw_ref[...]).astype(g_ref.dtype)

    @pl.when(c == pl.num_programs(1) - 1)
    def _():
        sout_ref[0] = s_ref[...]


def ssd_scan(xbc, dt, z, s0, params, *, n_rows, n_first, name):
    nb, _, conv_dim = xbc.shape
    t = n_rows
    d_inner = z.shape[2]
    d_state = s0.shape[2]
    n_heads = params["n_heads"]
    n_groups = (conv_dim - d_inner) // (2 * d_state)
    assert t % SSD_CHUNK == 0 and (d_inner // n_heads) * 2 == LANES
    nc = t // SSD_CHUNK
    L = SSD_CHUNK
    const = lambda b, c: (0, 0)
    body = functools.partial(_ssd_body, n_first=n_first, d_inner=d_inner, n_groups=n_groups,
                             d_state=d_state, n_heads=n_heads)
    return pl.pallas_call(
        body,
        grid=(nb, nc),
        in_specs=[
            pl.BlockSpec((1, L, conv_dim), lambda b, c: (b, c, 0)),
            pl.BlockSpec((1, L, LANES), lambda b, c: (b, c, 0)),
            pl.BlockSpec((1, L, d_inner), lambda b, c: (b, c, 0)),
            pl.BlockSpec((1, d_inner, d_state), lambda b, c: (b, 0, 0)),
            pl.BlockSpec(params["conv_w"].shape, const),
            pl.BlockSpec(params["conv_b"].shape, const),
            pl.BlockSpec(params["dt_bias"].shape, const),
            pl.BlockSpec(params["a_log"].shape, const),
            pl.BlockSpec(params["d_skip"].shape, const),
            pl.BlockSpec(params["norm_w"].shape, const),
            pl.BlockSpec(params["e_sel"].shape, const),
            pl.BlockSpec(params["e_sel_t"].shape, const),
        ],
        out_specs=[
            pl.BlockSpec((1, L, d_inner), lambda b, c: (b, c, 0)),
            pl.BlockSpec((1, d_inner, d_state), lambda b, c: (b, 0, 0)),
        ],
        out_shape=[
            jax.ShapeDtypeStruct((nb, t, d_inner), BF16),
            jax.ShapeDtypeStruct((nb, d_inner, d_state), F32),
        ],
        scratch_shapes=[
            pltpu.VMEM((L + SUBLANES, conv_dim), F32),
            pltpu.VMEM((d_inner, d_state), F32),
            pltpu.VMEM((L, d_inner), F32),
        ],
        compiler_params=pltpu.CompilerParams(
            dimension_semantics=("arbitrary", "arbitrary"), vmem_limit_bytes=VMEM_LIMIT),
        name=name,
    )(xbc, dt, z, s0, params["conv_w"], params["conv_b"], params["dt_bias"], params["a_log"],
      params["d_skip"], params["norm_w"], params["e_sel"], params["e_sel_t"])


def _sb_block(z, vis, carry, u_strict):
    nz = -z
    lk = jnp.minimum(nz, 0.0) - jnp.log2(1.0 + jnp.exp2(jnp.minimum(z, nz)))
    lkm = lk if vis is None else jnp.where(vis, lk, 0.0)
    hi = lkm.astype(BF16)
    lo = (lkm - hi.astype(F32)).astype(BF16)
    later = _dot(hi, u_strict) + _dot(lo, u_strict)
    w = jnp.exp2(z + lk + later + carry)
    if vis is not None:
        w = jnp.where(vis, w, 0.0)
    return w, carry + jnp.sum(lkm, axis=1, keepdims=True)


def _attn_prompt_body(bias_ref, q_ref, k_ref, v_ref, o_ref, *, n_pad, hd):
    pair = pl.program_id(0)
    qi = pl.program_id(1)
    tq, tk = ATTN_TQ, ATTN_TK
    lane = lax.broadcasted_iota(I32, (tq, LANES), 1)
    lo_half = lane < hd
    kr = lax.broadcasted_iota(I32, (tk, tk), 0)
    kc = lax.broadcasted_iota(I32, (tk, tk), 1)
    u_strict = (kr > kc).astype(BF16)
    qpos = qi * tq + lax.broadcasted_iota(I32, (tq, tk), 0)
    kidx = lax.broadcasted_iota(I32, (tq, tk), 1)
    n_blocks = (qi + 1) * (tq // tk)
    q = q_ref[...]
    accs = []
    for hh in range(2):
        qh = jnp.where(lo_half, q, 0) if hh == 0 else jnp.where(lo_half, 0, q)
        b2 = bias_ref[2 * pair + hh]

        def step(t, st):
            carry, acc = st
            kb = n_blocks - 1 - t
            k0 = pl.multiple_of(kb * tk, tk)
            kblk = k_ref[pl.ds(k0, tk), :]
            vblk = v_ref[pl.ds(k0, tk), :]
            z = _dot_nt(qh, kblk) + b2
            kpos = k0 + kidx
            vis = (kpos < qpos) & (kpos >= n_pad)
            w, carry = _sb_block(z, vis, carry, u_strict)
            return carry, acc + _dot(w.astype(BF16), vblk)

        _, acc = lax.fori_loop(0, n_blocks, step,
                               (jnp.zeros((tq, 1), F32), jnp.zeros((tq, LANES), F32)))
        accs.append(acc)
    o_ref[...] = jnp.where(lo_half, accs[0], accs[1]).astype(o_ref.dtype)


def attn_prompt(qb, kb, vb, bias2, *, rows, n_pad, hd):
    t, d = qb.shape
    n_pairs = d // LANES
    body = functools.partial(_attn_prompt_body, n_pad=n_pad, hd=hd)
    return pl.pallas_call(
        body,
        grid_spec=pltpu.PrefetchScalarGridSpec(
            num_scalar_prefetch=1,
            grid=(n_pairs, rows // ATTN_TQ),
            in_specs=[
                pl.BlockSpec((ATTN_TQ, LANES), lambda p, i, b: (i, p)),
                pl.BlockSpec((rows, LANES), lambda p, i, b: (0, p)),
                pl.BlockSpec((rows, LANES), lambda p, i, b: (0, p)),
            ],
            out_specs=pl.BlockSpec((ATTN_TQ, LANES), lambda p, i, b: (i, p)),
        ),
        out_shape=jax.ShapeDtypeStruct((rows, d), BF16),
        compiler_params=pltpu.CompilerParams(
            dimension_semantics=("arbitrary", "arbitrary"), vmem_limit_bytes=VMEM_LIMIT),
        name="attn_prompt",
    )(bias2, qb, kb, vb)


def _attn_sample_body(pt_ref, q_ref, kn_ref, vn_ref, bias_ref, *rest, n_heads, hd, pps, dec):
    k_refs = rest[:pps]
    v_refs = rest[pps:2 * pps]
    o_ref = rest[2 * pps]
    qbd_ref, kpad_ref, vpad_ref, acc_ref, carry_ref = rest[2 * pps + 1:]
    j = pl.program_id(1)
    nr = n_heads * dec
    ps = k_refs[0].shape[1]
    kr = lax.broadcasted_iota(I32, (ps, ps), 0)
    kc = lax.broadcasted_iota(I32, (ps, ps), 1)
    u_strict = (kr > kc).astype(BF16)
    b2 = bias_ref[...]

    def process(kblk, vblk, vis):
        z = _dot_nt(qbd_ref[...], kblk) + b2
        w, carry = _sb_block(z, vis, carry_ref[...], u_strict)
        carry_ref[...] = carry
        acc_ref[...] += _dot(w.astype(BF16), vblk)

    @pl.when(j == 0)
    def _():
        d = q_ref.shape[1]
        qt = jnp.concatenate([q_ref[...]] * n_heads, axis=0)
        rh = lax.broadcasted_iota(I32, (nr, d), 0) // dec
        lh = lax.broadcasted_iota(I32, (nr, d), 1) // hd
        qbd_ref[...] = jnp.where(rh == lh, qt, 0.0).astype(BF16)
        kpad_ref[...] = jnp.zeros(kpad_ref.shape, F32)
        vpad_ref[...] = jnp.zeros(vpad_ref.shape, F32)
        kpad_ref[0:dec, :] = kn_ref[...]
        vpad_ref[0:dec, :] = vn_ref[...]
        acc_ref[...] = jnp.zeros(acc_ref.shape, F32)
        carry_ref[...] = jnp.zeros(carry_ref.shape, F32)
        qrow = lax.broadcasted_iota(I32, (nr, ps), 0) % dec
        kidx = lax.broadcasted_iota(I32, (nr, ps), 1)
        process(kpad_ref[...].astype(BF16), vpad_ref[...].astype(BF16), kidx < qrow)

    @pl.when(j > 0)
    def _():
        for i in range(pps - 1, -1, -1):
            process(k_refs[i][0].astype(BF16), v_refs[i][0].astype(BF16), None)

    @pl.when(j == pl.num_programs(1) - 1)
    def _():
        d = o_ref.shape[1]
        lh = lax.broadcasted_iota(I32, (dec, d), 1) // hd
        out = jnp.zeros((dec, d), F32)
        for h in range(n_heads):
            out = out + jnp.where(lh == h, acc_ref[h * dec:(h + 1) * dec, :], 0.0)
        o_ref[...] = out.astype(o_ref.dtype)


def attn_sample(q, k, v, row0, cache_k, cache_v, page_table, bias_col, *, n_heads, hd):
    nb, n_pages = page_table.shape
    n_pool, ps = cache_k.shape[0], cache_k.shape[1]
    d = n_heads * hd
    dec = SUBLANES
    nr = n_heads * dec
    assert nr == ps == LANES and row0 % dec == 0
    pps = min(PAGES_PER_STEP, n_pages)
    assert n_pages % pps == 0
    n_steps = n_pages // pps
    ck = cache_k.reshape(n_pool, ps, d)
    cv = cache_v.reshape(n_pool, ps, d)
    pt = page_table.reshape(-1).astype(I32)
    rb0 = row0 // dec

    def page_map(b, j, pt_ref, *, i):
        jj = jnp.maximum(j, 1)
        return (pt_ref[b * n_pages + n_pages - jj * pps + i], 0, 0)

    new_map = lambda b, j, pt_ref: (rb0 + b, 0)
    in_specs = [
        pl.BlockSpec((dec, d), new_map),
        pl.BlockSpec((dec, d), new_map),
        pl.BlockSpec((dec, d), new_map),
        pl.BlockSpec((nr, 1), lambda b, j, pt_ref: (0, 0)),
    ]
    in_specs += [pl.BlockSpec((1, ps, d), functools.partial(page_map, i=i)) for i in range(pps)]
    in_specs += [pl.BlockSpec((1, ps, d), functools.partial(page_map, i=i)) for i in range(pps)]
    body = functools.partial(_attn_sample_body, n_heads=n_heads, hd=hd, pps=pps, dec=dec)
    return pl.pallas_call(
        body,
        grid_spec=pltpu.PrefetchScalarGridSpec(
            num_scalar_prefetch=1,
            grid=(nb, n_steps + 1),
            in_specs=in_specs,
            out_specs=pl.BlockSpec((dec, d), lambda b, j, pt_ref: (b, 0)),
            scratch_shapes=[
                pltpu.VMEM((nr, d), BF16),
                pltpu.VMEM((ps, d), F32),
                pltpu.VMEM((ps, d), F32),
                pltpu.VMEM((nr, d), F32),
                pltpu.VMEM((nr, 1), F32),
            ],
        ),
        out_shape=jax.ShapeDtypeStruct((nb * dec, d), BF16),
        compiler_params=pltpu.CompilerParams(
            dimension_semantics=("arbitrary", "arbitrary"), vmem_limit_bytes=VMEM_LIMIT),
        name="attn_sample",
    )(pt, q, k, v, bias_col, *([ck] * pps), *([cv] * pps))


def _router_body(*refs, n_add, scale_lanes, n_groups, per_group):
    it = iter(refs)
    add_refs = [next(it) for _ in range(n_add)]
    sc_ref = next(it) if any(l is not None for l in scale_lanes) else None
    nw_ref = next(it)
    wr_ref = next(it)
    x_ref, hn_ref, idx_ref, wt_ref = next(it), next(it), next(it), next(it)

    x = None
    for r, ln in zip(add_refs, scale_lanes):
        v = r[...]
        if ln is not None:
            v = v * sc_ref[:, ln:ln + 1]
        x = v if x is None else x + v
    x_ref[...] = x
    ms = jnp.mean(x * x, axis=-1, keepdims=True)
    hn = (x * lax.rsqrt(ms + RMS_EPS)) * nw_ref[...]
    hn_ref[...] = hn
    logits = jnp.dot(hn, wr_ref[...], preferred_element_type=F32, precision=lax.Precision.HIGHEST)

    tm = logits.shape[0]
    lane = lax.broadcasted_iota(I32, (tm, LANES), 1)
    far = jnp.int32(4 * LANES)
    gl = jnp.where(lane < n_groups, logits, NEG_BIG)
    gmax = jnp.max(gl, axis=1, keepdims=True)
    gidx = jnp.min(jnp.where(gl == gmax, lane, far), axis=1, keepdims=True)
    gsum = jnp.sum(jnp.where(lane < n_groups, jnp.exp(logits - gmax), 0.0), axis=1, keepdims=True)
    g_w = 1.0 / gsum
    lo = n_groups + gidx * per_group
    el = jnp.where(lane >= lo, jnp.where(lane < lo + per_group, logits, NEG_BIG), NEG_BIG)
    v1 = jnp.max(el, axis=1, keepdims=True)
    i1 = jnp.min(jnp.where(el == v1, lane, far), axis=1, keepdims=True)
    el2 = jnp.where(lane == i1, NEG_BIG, el)
    v2 = jnp.max(el2, axis=1, keepdims=True)
    i2 = jnp.min(jnp.where(el2 == v2, lane, far), axis=1, keepdims=True)
    e2 = jnp.exp(v2 - v1)
    den = 1.0 + e2
    w1 = (1.0 / den) * g_w
    w2 = (e2 / den) * g_w
    idx_ref[...] = jnp.where(lane == 0, i1 - n_groups, jnp.where(lane == 1, i2 - n_groups, 0))
    wt_ref[...] = jnp.where(lane == 0, w1, jnp.where(lane == 1, w2, 0.0))


def moe_router(addends, *, rows, scale, scale_lanes, norm_w, w_route, n_groups, per_group, tm=256):
    n_add = len(addends)
    d = addends[0][0].shape[1]
    in_specs, args = [], []
    for arr, row0 in addends:
        in_specs.append(pl.BlockSpec((tm, d), functools.partial(lambda i, b: (i + b, 0), b=row0 // tm)))
        args.append(arr)
    if scale is not None:
        in_specs.append(pl.BlockSpec((tm, LANES), lambda i: (i, 0)))
        args.append(scale)
    in_specs.append(pl.BlockSpec((1, d), lambda i: (0, 0)))
    args.append(norm_w.reshape(1, d).astype(F32))
    in_specs.append(pl.BlockSpec(w_route.shape, lambda i: (0, 0)))
    args.append(w_route)
    row_spec = pl.BlockSpec((tm, d), lambda i: (i, 0))
    lane_spec = pl.BlockSpec((tm, LANES), lambda i: (i, 0))
    body = functools.partial(_router_body, n_add=n_add, scale_lanes=tuple(scale_lanes),
                             n_groups=n_groups, per_group=per_group)
    return pl.pallas_call(
        body,
        grid=(rows // tm,),
        in_specs=in_specs,
        out_specs=[row_spec, row_spec, lane_spec, lane_spec],
        out_shape=[
            jax.ShapeDtypeStruct((rows, d), F32),
            jax.ShapeDtypeStruct((rows, d), F32),
            jax.ShapeDtypeStruct((rows, LANES), I32),
            jax.ShapeDtypeStruct((rows, LANES), F32),
        ],
        compiler_params=pltpu.CompilerParams(
            dimension_semantics=("arbitrary",), vmem_limit_bytes=VMEM_LIMIT),
        name="moe_router",
    )(*args)


def _experts_body(te_ref, nv_ref, cnt_ref, src_ref, dst_ref, hn_ref, w1_ref, w3_ref, w2_ref, y_ref,
                  xbuf, ybuf, gsem, ssem):
    i = pl.program_id(0)
    nt = pl.num_programs(0)
    tm = xbuf.shape[1]
    n_valid = nv_ref[0]
    slot = i % 2

    def start_gather(tile, s):
        def issue(r, carry):
            tok = src_ref[tile * tm + r]
            pltpu.make_async_copy(hn_ref.at[pl.ds(tok, 1)], xbuf.at[s, pl.ds(r, 1)], gsem.at[s]).start()
            return carry
        lax.fori_loop(0, tm, issue, 0)

    def wait_rows(sem, s, buf, n):
        def wait_full():
            pltpu.make_async_copy(hn_ref.at[pl.ds(0, tm)], buf.at[s], sem.at[s]).wait()

        if isinstance(n, int):
            assert n == tm
            wait_full()
            return
        pl.when(n == tm)(wait_full)

        @pl.when(n < tm)
        def _():
            def one(r, carry):
                pltpu.make_async_copy(hn_ref.at[pl.ds(0, 1)], buf.at[s, pl.ds(0, 1)], sem.at[s]).wait()
                return carry
            lax.fori_loop(0, n, one, 0)

    @pl.when(i == 0)
    def _():
        start_gather(0, 0)

    @pl.when(i + 1 < n_valid)
    def _():
        start_gather(i + 1, 1 - slot)

    @pl.when(i < n_valid)
    def _():
        wait_rows(gsem, slot, xbuf, tm)
        x = xbuf[slot].astype(BF16)
        a = _dot(x, w1_ref[0])
        u = _dot(x, w3_ref[0])
        hid = (_silu(a) * u).astype(BF16)
        y = _dot(hid, w2_ref[0])

        @pl.when(i >= 2)
        def _():
            wait_rows(ssem, slot, ybuf, cnt_ref[jnp.maximum(i - 2, 0)])

        ybuf[slot] = y

        def issue(r, carry):
            row = dst_ref[i * tm + r]
            pltpu.make_async_copy(ybuf.at[slot, pl.ds(r, 1)], y_ref.at[pl.ds(row, 1)], ssem.at[slot]).start()
            return carry
        lax.fori_loop(0, cnt_ref[i], issue, 0)

    @pl.when(i == nt - 1)
    def _():
        wait_rows(ssem, (n_valid - 1) % 2, ybuf, cnt_ref[n_valid - 1])

        @pl.when(n_valid >= 2)
        def _():
            wait_rows(ssem, n_valid % 2, ybuf, cnt_ref[jnp.maximum(n_valid - 2, 0)])


def moe_experts(hn, tile_expert, n_valid, tile_rows, src_tok, dst_row, w1, w3, w2, *, out_rows):
    n_tiles = tile_expert.shape[0]
    d = hn.shape[1]
    tm = MOE_TM
    wmap = lambda i, te, nv, cnt, s, dd: (te[i], 0, 0)
    return pl.pallas_call(
        _experts_body,
        grid_spec=pltpu.PrefetchScalarGridSpec(
            num_scalar_prefetch=5,
            grid=(n_tiles,),
            in_specs=[
                pl.BlockSpec(memory_space=pl.ANY),
                pl.BlockSpec((1,) + w1.shape[1:], wmap),
                pl.BlockSpec((1,) + w3.shape[1:], wmap),
                pl.BlockSpec((1,) + w2.shape[1:], wmap),
            ],
            out_specs=pl.BlockSpec(memory_space=pl.ANY),
            scratch_shapes=[
                pltpu.VMEM((2, tm, d), F32),
                pltpu.VMEM((2, tm, d), F32),
                pltpu.SemaphoreType.DMA((2,)),
                pltpu.SemaphoreType.DMA((2,)),
            ],
        ),
        out_shape=jax.ShapeDtypeStruct((out_rows, d), F32),
        compiler_params=pltpu.CompilerParams(
            dimension_semantics=("arbitrary",), vmem_limit_bytes=VMEM_LIMIT),
        name="moe_experts",
    )(tile_expert, n_valid, tile_rows, src_tok, dst_row, hn, w1, w3, w2)


def _dispatch_plan(eid, n_experts, n_tiles):
    t = eid.shape[0]
    tm = MOE_TM
    e_flat = eid.reshape(-1)
    onehot = (e_flat[:, None] == jnp.arange(n_experts, dtype=I32)[None, :]).astype(I32)
    rank = jnp.sum((jnp.cumsum(onehot, axis=0) - onehot) * onehot, axis=1)
    counts = jnp.sum(onehot, axis=0)
    padded = (counts + tm - 1) // tm * tm
    ends = jnp.cumsum(padded)
    offs = ends - padded
    pos = offs[e_flat] + rank
    n_sorted = n_tiles * tm
    tok = jnp.arange(t * TOP_K, dtype=I32) // TOP_K
    slot = jnp.arange(t * TOP_K, dtype=I32) % TOP_K
    src_tok = jnp.zeros((n_sorted,), I32).at[pos].set(tok)
    dst_row = jnp.zeros((n_sorted,), I32).at[pos].set(slot * t + tok)
    n_valid = (ends[-1] // tm).astype(I32)
    tile_start = jnp.arange(n_tiles, dtype=I32) * tm
    tile_expert = jnp.sum((tile_start[:, None] >= ends[None, :]).astype(I32), axis=1)
    last_e = jnp.sum((jnp.maximum(ends[-1] - tm, 0) >= ends).astype(I32))
    tile_expert = jnp.minimum(tile_expert, last_e).astype(I32)
    tile_rows = jnp.clip(counts[tile_expert] - (tile_start - offs[tile_expert]), 0, tm)
    tile_rows = jnp.where(tile_start < ends[-1], tile_rows, 0).astype(I32)
    return tile_expert, n_valid.reshape(1), tile_rows, src_tok, dst_row


def moe_layer(addends, scale, scale_lanes, norm_w, w_group, w_expert, w1, w3, w2, *, rows):
    d = w_group.shape[0]
    n_groups = w_group.shape[1]
    n_experts = w_expert.shape[1]
    w_route = jnp.zeros((d, LANES), F32).at[:, :n_groups].set(w_group).at[:, n_groups:n_groups + n_experts].set(w_expert)
    x, hn, idx, wts = moe_router(addends, rows=rows, scale=scale, scale_lanes=scale_lanes, norm_w=norm_w,
                                 w_route=w_route, n_groups=n_groups, per_group=n_experts // n_groups)
    n_tiles = (rows * TOP_K + n_experts * (MOE_TM - 1) + MOE_TM - 1) // MOE_TM
    plan = _dispatch_plan(idx[:, :TOP_K], n_experts, n_tiles)
    y2 = moe_experts(hn, *plan, w1.astype(BF16), w3.astype(BF16), w2.astype(BF16),
                     out_rows=TOP_K * rows)
    return x, wts, y2


def kernel(x_prompt, x_sample, state_ssm, state_conv, cache_k, cache_v, page_table, meta_tokens,
           norm_mix, norm_ffn, norm_final, m_w_in, m_conv_w, m_conv_b, m_dt_bias, m_A_log, m_D,
           m_norm_w, m_w_out, a_w_qkv, a_w_o, a_logit_bias, moe_w_group, moe_w_expert, moe_w1, moe_w3, moe_w2):
    bp, seq, d = x_prompt.shape
    nb, dec, _ = x_sample.shape
    n_meta = meta_tokens.shape[0]
    assert bp == 1 and dec == SUBLANES
    d_inner = m_w_out.shape[0]
    n_heads_ssm = m_A_log.shape[0]
    conv_dim = m_conv_w.shape[1]
    d_state = state_ssm.shape[-1]
    hd_ssm = d_inner // n_heads_ssm
    n_heads = a_logit_bias.shape[0]
    hd = d // n_heads

    p_len = n_meta + seq
    n_pad = (-p_len) % SSD_CHUNK
    tp = n_pad + p_len
    tq = _round_up(tp, ATTN_TQ)
    ns = nb * dec
    row_s = tq
    rows = _round_up(row_s + ns, ROW_TILE)

    x0 = jnp.concatenate([
        jnp.zeros((n_pad, d), F32), meta_tokens.astype(F32), x_prompt[0],
        jnp.zeros((row_s - tp, d), F32), x_sample.reshape(ns, d),
        jnp.zeros((rows - row_s - ns, d), F32)], axis=0)

    w_in = m_w_in.astype(BF16)
    w_z = w_in[:, :d_inner]
    w_xbc = w_in[:, d_inner:d_inner + conv_dim]
    w_dt = jnp.zeros((d, LANES), BF16).at[:, :n_heads_ssm].set(w_in[:, d_inner + conv_dim:])
    z, xbc, dtr = fused_linear([(x0, 0)], rows=rows, norm_w=norm_mix[0], weights=(w_z, w_xbc, w_dt),
                               out_dtypes=((F32,), (F32,), (F32,)), name="in_proj")

    head_of_lane = jnp.arange(d_inner, dtype=I32) // hd_ssm
    e_sel = (jnp.arange(LANES, dtype=I32)[:, None] == head_of_lane[None, :]).astype(BF16)
    pad_h = LANES - n_heads_ssm
    ssd_params = dict(
        n_heads=n_heads_ssm,
        conv_w=m_conv_w.astype(F32), conv_b=m_conv_b.reshape(1, conv_dim).astype(F32),
        dt_bias=jnp.pad(m_dt_bias.astype(F32), (0, pad_h)).reshape(1, LANES),
        a_log=jnp.pad(m_A_log.astype(F32), (0, pad_h)).reshape(1, LANES),
        d_skip=jnp.repeat(m_D.astype(F32), hd_ssm).reshape(1, d_inner),
        norm_w=m_norm_w.reshape(1, d_inner).astype(F32),
        e_sel=e_sel, e_sel_t=e_sel.T)
    g_p, ssm_p = ssd_scan(xbc[None], dtr[None], z[None], jnp.zeros((1, d_inner, d_state), F32),
                          ssd_params, n_rows=tp, n_first=n_pad, name="ssd_prompt")

    L = SSD_CHUNK
    kw = state_conv.shape[1]
    lead = L - dec - kw

    def seq_chunk(a, pre):
        a = a[row_s:row_s + ns].reshape(nb, dec, a.shape[1])
        return jnp.concatenate([jnp.zeros((nb, lead, a.shape[2]), F32), pre, a], axis=1)

    xbc_s = seq_chunk(xbc, state_conv.astype(F32))
    dt_s = seq_chunk(dtr, jnp.zeros((nb, kw, LANES), F32))
    z_s = seq_chunk(z, jnp.zeros((nb, kw, d_inner), F32))
    g_s, ssm_s = ssd_scan(xbc_s, dt_s, z_s, state_ssm.reshape(nb, d_inner, d_state).astype(F32),
                          ssd_params, n_rows=L, n_first=L - dec, name="ssd_sample")

    g = jnp.concatenate([g_p[0], jnp.zeros((row_s - tp, d_inner), BF16), g_s[:, L - dec:].reshape(ns, d_inner),
                         jnp.zeros((rows - row_s - ns, d_inner), BF16)], axis=0)
    (x1,) = fused_linear([(g, 0)], rows=rows, weights=(m_w_out.astype(BF16),), out_dtypes=((F32,),),
                         residual=x0, name="out_proj")

    conv_p = xbc[tp - kw:tp][None]
    conv_s = xbc[row_s:row_s + ns].reshape(nb, dec, conv_dim)[:, dec - kw:]

    x1, wts0, y0 = moe_layer([(x1, 0)], None, (None,), norm_ffn[0], moe_w_group[0], moe_w_expert[0],
                             moe_w1[0], moe_w3[0], moe_w2[0], rows=rows)

    qscale = (hd ** -0.5) * LOG2E
    w_q = (a_w_qkv[:, :d] * qscale).astype(BF16)
    w_k = a_w_qkv[:, d:2 * d].astype(BF16)
    w_v = a_w_qkv[:, 2 * d:].astype(BF16)
    x2, qf, qb, kf, kb, vf, vb = fused_linear(
        [(x1, 0), (y0, 0), (y0, rows)], rows=rows, scale=wts0, scale_lanes=(None, 0, 1), norm_w=norm_mix[1],
        weights=(w_q, w_k, w_v), out_dtypes=((F32, BF16), (F32, BF16), (F32, BF16)), emit_x=True, name="qkv_proj")
    bias2 = a_logit_bias.astype(F32) * LOG2E
    o_p = attn_prompt(qb, kb, vb, bias2, rows=tq, n_pad=n_pad, hd=hd)
    bias_col = jnp.repeat(bias2, dec).reshape(n_heads * dec, 1)
    o_s = attn_sample(qf, kf, vf, row_s, cache_k, cache_v, page_table, bias_col, n_heads=n_heads, hd=hd)
    o = jnp.concatenate([o_p, o_s, jnp.zeros((rows - row_s - ns, d), BF16)], axis=0)
    (x3,) = fused_linear([(o, 0)], rows=rows, weights=(a_w_o.astype(BF16),), out_dtypes=((F32,),),
                         residual=x2, name="attn_out_proj")

    x3, wts1, y1 = moe_layer([(x3, 0)], None, (None,), norm_ffn[1], moe_w_group[1], moe_w_expert[1],
                             moe_w1[1], moe_w3[1], moe_w2[1], rows=rows)

    (yn,) = fused_linear([(x3, 0), (y1, 0), (y1, rows)], rows=rows, scale=wts1, scale_lanes=(None, 0, 1),
                         norm_w=norm_final, emit_norm=True, name="final_norm")

    y_prompt = yn[n_pad + n_meta:tp][None]
    y_sample = yn[row_s:row_s + ns].reshape(nb, dec, d)
    k_p = kf[n_pad:tp].reshape(1, p_len, n_heads, hd)
    v_p = vf[n_pad:tp].reshape(1, p_len, n_heads, hd)
    k_s = kf[row_s:row_s + ns].reshape(nb, dec, n_heads, hd)
    v_s = vf[row_s:row_s + ns].reshape(nb, dec, n_heads, hd)
    return (y_prompt, y_sample, ssm_p.reshape(1, n_heads_ssm, hd_ssm, d_state), conv_p,
            k_p, v_p, ssm_s.reshape(nb, n_heads_ssm, hd_ssm, d_state), conv_s, k_s, v_s)
```

```python
import functools
import math

import jax
import jax.numpy as jnp
from jax import lax
from jax.experimental import pallas as pl
from jax.experimental.pallas import tpu as pltpu

F32 = jnp.float32
BF16 = jnp.bfloat16
I32 = jnp.int32

RMS_EPS = 1e-6
LOG2E = 1.4426950408889634
NEG_BIG = -3.0e38

LANES = 128
SUBLANES = 8
VMEM_LIMIT = 48 * 1024 * 1024

SSD_CHUNK = 128
ROW_TILE = 512
ATTN_TQ = 256
ATTN_TK = 128
MOE_TM = 256
TOP_K = 2
PAGES_PER_STEP = 8


def _round_up(x, m):
    return (x + m - 1) // m * m


def _dot(a, b):
    return jnp.dot(a, b, preferred_element_type=F32)


def _dot_nt(a, b):
    return lax.dot_general(a, b, (((1,), (1,)), ((), ())), preferred_element_type=F32)


def _split3(x):
    hi = x.astype(BF16)
    r1 = x - hi.astype(F32)
    mid = r1.astype(BF16)
    lo = (r1 - mid.astype(F32)).astype(BF16)
    return hi, mid, lo


def _dot_exact_rhs(sel, x):
    hi, mid, lo = _split3(x)
    return _dot(sel, hi) + _dot(sel, mid) + _dot(sel, lo)


def _dot_exact_lhs(x, sel):
    hi, mid, lo = _split3(x)
    return _dot(hi, sel) + _dot(mid, sel) + _dot(lo, sel)


def _silu(x):
    return x / (1.0 + jnp.exp(-x))


def _softplus(x):
    return jnp.maximum(x, 0.0) + jnp.log1p(jnp.exp(-jnp.abs(x)))


def _linear_body(*refs, n_add, scale_lanes, has_scale, has_norm, n_w, has_res,
                 emit_x, emit_norm, n_outs, tn):
    it = iter(refs)
    add_refs = [next(it) for _ in range(n_add)]
    sc_ref = next(it) if has_scale else None
    nw_ref = next(it) if has_norm else None
    w_refs = [next(it) for _ in range(n_w)]
    res_ref = next(it) if has_res else None
    xo_ref = next(it) if emit_x else None
    no_ref = next(it) if emit_norm else None
    out_refs = [[next(it) for _ in range(k)] for k in n_outs]
    xb_ref = next(it) if n_w else None

    x = None
    for r, lane in zip(add_refs, scale_lanes):
        v = r[...]
        if lane is not None:
            v = v * sc_ref[:, lane:lane + 1]
        x = v if x is None else x + v
    if emit_x:
        xo_ref[...] = x
    if has_norm:
        ms = jnp.mean(x * x, axis=-1, keepdims=True)
        x = (x * lax.rsqrt(ms + RMS_EPS)) * nw_ref[...]
        if emit_norm:
            no_ref[...] = x
    if not n_w:
        return
    xb_ref[...] = x.astype(BF16)
    for k, w_ref in enumerate(w_refs):
        n_cols = w_ref.shape[1]
        for n0 in range(0, n_cols, tn):
            n1 = min(n0 + tn, n_cols)
            r = _dot(xb_ref[...], w_ref[:, n0:n1])
            if has_res and k == 0:
                r = r + res_ref[:, n0:n1]
            for o_ref in out_refs[k]:
                o_ref[:, n0:n1] = r.astype(o_ref.dtype)


def fused_linear(addends, *, rows, scale=None, scale_lanes=None, norm_w=None, weights=(),
                 out_dtypes=(), residual=None, emit_x=False, emit_norm=False,
                 tm=256, tn=512, name="fused_linear"):
    n_add = len(addends)
    if scale_lanes is None:
        scale_lanes = (None,) * n_add
    d_in = addends[0][0].shape[1]
    assert rows % tm == 0
    in_specs, args = [], []
    for arr, row0 in addends:
        assert row0 % tm == 0
        in_specs.append(pl.BlockSpec((tm, d_in), functools.partial(lambda i, b: (i + b, 0), b=row0 // tm)))
        args.append(arr)
    if scale is not None:
        in_specs.append(pl.BlockSpec((tm, LANES), lambda i: (i, 0)))
        args.append(scale)
    if norm_w is not None:
        in_specs.append(pl.BlockSpec((1, d_in), lambda i: (0, 0)))
        args.append(norm_w.reshape(1, d_in).astype(F32))
    for w in weights:
        in_specs.append(pl.BlockSpec(w.shape, lambda i: (0, 0)))
        args.append(w)
    if residual is not None:
        in_specs.append(pl.BlockSpec((tm, residual.shape[1]), lambda i: (i, 0)))
        args.append(residual)
    out_shapes, out_specs = [], []
    if emit_x:
        out_shapes.append(jax.ShapeDtypeStruct((rows, d_in), F32))
        out_specs.append(pl.BlockSpec((tm, d_in), lambda i: (i, 0)))
    if emit_norm:
        out_shapes.append(jax.ShapeDtypeStruct((rows, d_in), F32))
        out_specs.append(pl.BlockSpec((tm, d_in), lambda i: (i, 0)))
    for w, dts in zip(weights, out_dtypes):
        for dt in dts:
            out_shapes.append(jax.ShapeDtypeStruct((rows, w.shape[1]), dt))
            out_specs.append(pl.BlockSpec((tm, w.shape[1]), lambda i: (i, 0)))
    scratch = [pltpu.VMEM((tm, d_in), BF16)] if weights else []
    body = functools.partial(
        _linear_body, n_add=n_add, scale_lanes=tuple(scale_lanes), has_scale=scale is not None,
        has_norm=norm_w is not None, n_w=len(weights), has_res=residual is not None,
        emit_x=emit_x, emit_norm=emit_norm, n_outs=tuple(len(d) for d in out_dtypes), tn=tn)
    return pl.pallas_call(
        body,
        grid=(rows // tm,),
        in_specs=in_specs,
        out_specs=out_specs,
        out_shape=out_shapes,
        scratch_shapes=scratch,
        compiler_params=pltpu.CompilerParams(
            dimension_semantics=("arbitrary",), vmem_limit_bytes=VMEM_LIMIT),
        name=name,
    )(*args)


def _ssd_body(xbc_ref, dt_ref, z_ref, s0_ref, cw_ref, cb_ref, dtb_ref, alog_ref, dsk_ref,
              nw_ref, e_ref, et_ref, g_ref, sout_ref, ext_ref, s_ref, y_ref, *,
              n_first, d_inner, n_groups, d_state, n_heads):
    c = pl.program_id(1)
    L = SSD_CHUNK
    hd = d_inner // n_heads
    gw = n_groups * d_state
    hpg = n_heads // n_groups

    @pl.when(c == 0)
    def _():
        ext_ref[0:SUBLANES, :] = jnp.zeros((SUBLANES, ext_ref.shape[1]), F32)
        s_ref[...] = s0_ref[0]

    ext_ref[SUBLANES:SUBLANES + L, :] = xbc_ref[0]
    conv = cb_ref[...] + ext_ref[5:5 + L, :] * cw_ref[0:1, :]
    conv = conv + ext_ref[6:6 + L, :] * cw_ref[1:2, :]
    conv = conv + ext_ref[7:7 + L, :] * cw_ref[2:3, :]
    conv = conv + ext_ref[8:8 + L, :] * cw_ref[3:4, :]
    ext_ref[0:SUBLANES, :] = ext_ref[L:L + SUBLANES, :]

    row = lax.broadcasted_iota(I32, (L, 1), 0)
    n0 = jnp.where(c == 0, n_first, 0)
    valid = (row >= n0).astype(F32)
    xbc = _silu(conv) * valid
    xs = xbc[:, :d_inner]

    dt = _softplus(dt_ref[0] + dtb_ref[...]) * valid
    a = dt * (-jnp.exp(alog_ref[...]))
    ri = lax.broadcasted_iota(I32, (L, L), 0)
    ci = lax.broadcasted_iota(I32, (L, L), 1)
    tri = ri >= ci
    acs = _dot_exact_rhs(tri.astype(BF16), a)
    acs_t = acs.T
    acs_last = acs[L - 1:L, :]

    e_sel = e_ref[...]
    dt_e = _dot_exact_lhs(dt, e_sel)
    grow_e = _dot_exact_lhs(jnp.exp(acs), e_sel)
    dec_e = _dot_exact_lhs(jnp.exp(acs_last - acs), e_sel)
    xdt = xs * dt_e
    xw = xdt * dec_e

    lane = lax.broadcasted_iota(I32, (L, LANES), 1)
    lo_half = lane < hd
    s_new = []
    for g in range(n_groups):
        bg = xbc[:, d_inner + g * d_state:d_inner + (g + 1) * d_state].astype(BF16)
        cg = xbc[:, d_inner + gw + g * d_state:d_inner + gw + (g + 1) * d_state].astype(BF16)
        cb = _dot_nt(cg, bg)
        for j in range(hpg // 2):
            h0 = g * hpg + 2 * j
            ms = []
            for h in (h0, h0 + 1):
                diff = acs[:, h:h + 1] - acs_t[h:h + 1, :]
                ms.append((cb * jnp.where(tri, jnp.exp(diff), 0.0)).astype(BF16))
            lhs = jnp.concatenate(ms, axis=1)
            xp = xdt[:, h0 * hd:(h0 + 2) * hd]
            rhs = jnp.concatenate([jnp.where(lo_half, xp, 0.0), jnp.where(lo_half, 0.0, xp)],
                                  axis=0).astype(BF16)
            y_ref[:, h0 * hd:(h0 + 2) * hd] = _dot(lhs, rhs)
        c0, c1 = g * hpg * hd, (g + 1) * hpg * hd
        s_g = s_ref[c0:c1, :]
        y_off = _dot_nt(cg, s_g.astype(BF16)) * grow_e[:, c0:c1]
        y_ref[:, c0:c1] = y_ref[:, c0:c1] + y_off
        s_new.append(_dot(xw[:, c0:c1].T.astype(BF16), bg))

    last_col = jnp.broadcast_to(acs_t[:, L - 1:L], (LANES, LANES))
    cdec = jnp.exp(_dot_exact_rhs(et_ref[...], last_col))
    s_ref[...] = s_ref[...] * cdec + jnp.concatenate(s_new, axis=0)

    y = y_ref[...] + xs * dsk_ref[...]
    gt = y * _silu(z_ref[0])
    gsz = d_inner // n_groups
    outs = []
    for g in range(n_groups):
        gg = gt[:, g * gsz:(g + 1) * gsz]
        ms = jnp.mean(gg * gg, axis=-1, keepdims=True)
        outs.append(gg * lax.rsqrt(ms + RMS_EPS))
    g_ref[0] = (jnp.concatenate(outs, axis=1) * nw_ref[...]).astype(g_ref.dtype)

    @pl.when(c == pl.num_programs(1) - 1)
    def _():
        sout_ref[0] = s_ref[...]


def ssd_scan(xbc, dt, z, s0, params, *, n_rows, n_first, name):
    nb, _, conv_dim = xbc.shape
    t = n_rows
    d_inner = z.shape[2]
    d_state = s0.shape[2]
    n_heads = params["n_heads"]
    n_groups = (conv_dim - d_inner) // (2 * d_state)
    assert t % SSD_CHUNK == 0 and (d_inner // n_heads) * 2 == LANES
    nc = t // SSD_CHUNK
    L = SSD_CHUNK
    const = lambda b, c: (0, 0)
    body = functools.partial(_ssd_body, n_first=n_first, d_inner=d_inner, n_groups=n_groups,
                             d_state=d_state, n_heads=n_heads)
    return pl.pallas_call(
        body,
        grid=(nb, nc),
        in_specs=[
            pl.BlockSpec((1, L, conv_dim), lambda b, c: (b, c, 0)),
            pl.BlockSpec((1, L, LANES), lambda b, c: (b, c, 0)),
            pl.BlockSpec((1, L, d_inner), lambda b, c: (b, c, 0)),
            pl.BlockSpec((1, d_inner, d_state), lambda b, c: (b, 0, 0)),
            pl.BlockSpec(params["conv_w"].shape, const),
            pl.BlockSpec(params["conv_b"].shape, const),
            pl.BlockSpec(params["dt_bias"].shape, const),
            pl.BlockSpec(params["a_log"].shape, const),
            pl.BlockSpec(params["d_skip"].shape, const),
            pl.BlockSpec(params["norm_w"].shape, const),
            pl.BlockSpec(params["e_sel"].shape, const),
            pl.BlockSpec(params["e_sel_t"].shape, const),
        ],
        out_specs=[
            pl.BlockSpec((1, L, d_inner), lambda b, c: (b, c, 0)),
            pl.BlockSpec((1, d_inner, d_state), lambda b, c: (b, 0, 0)),
        ],
        out_shape=[
            jax.ShapeDtypeStruct((nb, t, d_inner), BF16),
            jax.ShapeDtypeStruct((nb, d_inner, d_state), F32),
        ],
        scratch_shapes=[
            pltpu.VMEM((L + SUBLANES, conv_dim), F32),
            pltpu.VMEM((d_inner, d_state), F32),
            pltpu.VMEM((L, d_inner), F32),
        ],
        compiler_params=pltpu.CompilerParams(
            dimension_semantics=("arbitrary", "arbitrary"), vmem_limit_bytes=VMEM_LIMIT),
        name=name,
    )(xbc, dt, z, s0, params["conv_w"], params["conv_b"], params["dt_bias"], params["a_log"],
      params["d_skip"], params["norm_w"], params["e_sel"], params["e_sel_t"])


MASKED_LOGIT = -1e30


def _suffix_sum_matrix(tk):
    kr = lax.broadcasted_iota(I32, (tk, tk), 0)
    kc = lax.broadcasted_iota(I32, (tk, tk), 1)
    return jnp.where(kr >= kc, -1.0, 0.0).astype(BF16)


def _sb_stage_a(z, vis):
    neg_abs = pltpu.bitcast(pltpu.bitcast(z, jnp.uint32) | jnp.uint32(0x80000000), F32)
    sp = jnp.maximum(z, 0.0) + jnp.log2(1.0 + jnp.exp2(neg_abs))
    if vis is not None:
        sp = jnp.where(vis, sp, 0.0)
        z = jnp.where(vis, z, MASKED_LOGIT)
    return z, sp.astype(BF16)


def _sb_stage_b(z, sp_split, carry, u2neg):
    suffix = _dot(sp_split, u2neg)
    return jnp.exp2(z + suffix + carry), carry + suffix[:, 0:1]


def _attn_prompt_body(q_ref, k_ref, v_ref, o_ref, z_scr, sp_scr, *, hd):
    qi = pl.program_id(1)
    tq, tk = ATTN_TQ, ATTN_TK
    u2neg = _suffix_sum_matrix(tk)
    lo_v = lax.broadcasted_iota(I32, (tk, LANES), 1) < hd
    qrow = qi * tq + lax.broadcasted_iota(I32, (tq, tk), 0)
    kidx = lax.broadcasted_iota(I32, (tq, tk), 1)
    qs = (q_ref[:, 0:LANES], q_ref[:, LANES:2 * LANES])

    def stage_a(kb, masked):
        k0 = pl.multiple_of(kb * tk, tk)
        kblk = k_ref[pl.ds(k0, tk), :]
        vis = (k0 + kidx) < qrow if masked else None
        return tuple(_sb_stage_a(_dot_nt(qs[hh], kblk[:, hh * LANES:(hh + 1) * LANES]), vis)
                     for hh in range(2))

    def stage_b(kb, st, carries, acc):
        k0 = pl.multiple_of(kb * tk, tk)
        vblk = v_ref[pl.ds(k0, tk), :]
        ws, cs = [], []
        for hh in range(2):
            w, c = _sb_stage_b(*st[hh], carries[hh], u2neg)
            ws.append(w.astype(BF16))
            cs.append(c)
        vcat = jnp.concatenate([jnp.where(lo_v, vblk, 0), jnp.where(lo_v, 0, vblk)], axis=0)
        return tuple(cs), acc + _dot(jnp.concatenate(ws, axis=1), vcat)

    nd = tq // tk
    n_full = qi * nd
    carries = (jnp.zeros((tq, 1), F32), jnp.zeros((tq, 1), F32))
    acc = jnp.zeros((tq, LANES), F32)
    def put(j, st):
        for hh in range(2):
            z_scr[j, hh] = st[hh][0]
            sp_scr[j, hh] = st[hh][1]

    def get(j):
        return tuple((z_scr[j, hh], sp_scr[j, hh]) for hh in range(2))

    for j in range(nd):
        put(j, stage_a(n_full + nd - 1 - j, True))

    def step(t, state):
        carries, acc = state
        kb0 = n_full - 1 - nd * t
        for j in range(nd):
            carries, acc = stage_b(kb0 + nd - j, get(j), carries, acc)
        for j in range(nd):
            put(j, stage_a(kb0 - j, False))
        return carries, acc

    carries, acc = lax.fori_loop(0, qi, step, (carries, acc))
    for j in range(nd):
        carries, acc = stage_b(nd - 1 - j, get(j), carries, acc)
    o_ref[...] = acc.astype(o_ref.dtype)


def attn_prompt(q_aug, k_aug, vb, *, rows, hd):
    d = vb.shape[1]
    n_pairs = d // LANES
    assert ATTN_TQ % ATTN_TK == 0
    body = functools.partial(_attn_prompt_body, hd=hd)
    return pl.pallas_call(
        body,
        grid=(n_pairs, rows // ATTN_TQ),
        in_specs=[
            pl.BlockSpec((ATTN_TQ, 2 * LANES), lambda p, i: (i, p)),
            pl.BlockSpec((rows, 2 * LANES), lambda p, i: (0, p)),
            pl.BlockSpec((rows, LANES), lambda p, i: (0, p)),
        ],
        out_specs=pl.BlockSpec((ATTN_TQ, LANES), lambda p, i: (i, p)),
        out_shape=jax.ShapeDtypeStruct((rows, d), BF16),
        scratch_shapes=[
            pltpu.VMEM((ATTN_TQ // ATTN_TK, 2, ATTN_TQ, ATTN_TK), F32),
            pltpu.VMEM((ATTN_TQ // ATTN_TK, 2, ATTN_TQ, ATTN_TK), BF16),
        ],
        compiler_params=pltpu.CompilerParams(
            dimension_semantics=("arbitrary", "arbitrary"), vmem_limit_bytes=VMEM_LIMIT),
        name="attn_prompt",
    )(q_aug, k_aug, vb)


def _attn_sample_body(pt_ref, q_ref, kn_ref, vn_ref, bias_ref, *rest, n_heads, hd, pps, dec):
    k_refs = rest[:pps]
    v_refs = rest[pps:2 * pps]
    o_ref = rest[2 * pps]
    qbd_ref, kpad_ref, vpad_ref, acc_ref, carry_ref = rest[2 * pps + 1:]
    j = pl.program_id(1)
    nr = n_heads * dec
    ps = k_refs[0].shape[1]
    u2neg = _suffix_sum_matrix(ps)
    b2 = bias_ref[...]

    def process(kblk, vblk, vis):
        z = _dot_nt(qbd_ref[...], kblk) + b2
        w, carry = _sb_stage_b(*_sb_stage_a(z, vis), carry_ref[...], u2neg)
        carry_ref[...] = carry
        acc_ref[...] += _dot(w.astype(BF16), vblk)

    @pl.when(j == 0)
    def _():
        d = q_ref.shape[1]
        qt = jnp.concatenate([q_ref[...]] * n_heads, axis=0)
        rh = lax.broadcasted_iota(I32, (nr, d), 0) // dec
        lh = lax.broadcasted_iota(I32, (nr, d), 1) // hd
        qbd_ref[...] = jnp.where(rh == lh, qt, 0.0).astype(BF16)
        kpad_ref[...] = jnp.zeros(kpad_ref.shape, F32)
        vpad_ref[...] = jnp.zeros(vpad_ref.shape, F32)
        kpad_ref[0:dec, :] = kn_ref[...]
        vpad_ref[0:dec, :] = vn_ref[...]
        acc_ref[...] = jnp.zeros(acc_ref.shape, F32)
        carry_ref[...] = jnp.zeros(carry_ref.shape, F32)
        qrow = lax.broadcasted_iota(I32, (nr, ps), 0) % dec
        kidx = lax.broadcasted_iota(I32, (nr, ps), 1)
        process(kpad_ref[...].astype(BF16), vpad_ref[...].astype(BF16), kidx < qrow)

    @pl.when(j > 0)
    def _():
        for i in range(pps - 1, -1, -1):
            process(k_refs[i][0].astype(BF16), v_refs[i][0].astype(BF16), None)

    @pl.when(j == pl.num_programs(1) - 1)
    def _():
        d = o_ref.shape[1]
        lh = lax.broadcasted_iota(I32, (dec, d), 1) // hd
        out = jnp.zeros((dec, d), F32)
        for h in range(n_heads):
            out = out + jnp.where(lh == h, acc_ref[h * dec:(h + 1) * dec, :], 0.0)
        o_ref[...] = out.astype(o_ref.dtype)


def attn_sample(q, k, v, row0, cache_k, cache_v, page_table, bias_col, *, n_heads, hd):
    nb, n_pages = page_table.shape
    n_pool, ps = cache_k.shape[0], cache_k.shape[1]
    d = n_heads * hd
    dec = SUBLANES
    nr = n_heads * dec
    assert nr == ps == LANES and row0 % dec == 0
    pps = min(PAGES_PER_STEP, n_pages)
    assert n_pages % pps == 0
    n_steps = n_pages // pps
    ck = cache_k.reshape(n_pool, ps, d)
    cv = cache_v.reshape(n_pool, ps, d)
    pt = page_table.reshape(-1).astype(I32)
    rb0 = row0 // dec

    def page_map(b, j, pt_ref, *, i):
        jj = jnp.maximum(j, 1)
        return (pt_ref[b * n_pages + n_pages - jj * pps + i], 0, 0)

    new_map = lambda b, j, pt_ref: (rb0 + b, 0)
    in_specs = [
        pl.BlockSpec((dec, d), new_map),
        pl.BlockSpec((dec, d), new_map),
        pl.BlockSpec((dec, d), new_map),
        pl.BlockSpec((nr, 1), lambda b, j, pt_ref: (0, 0)),
    ]
    in_specs += [pl.BlockSpec((1, ps, d), functools.partial(page_map, i=i)) for i in range(pps)]
    in_specs += [pl.BlockSpec((1, ps, d), functools.partial(page_map, i=i)) for i in range(pps)]
    body = functools.partial(_attn_sample_body, n_heads=n_heads, hd=hd, pps=pps, dec=dec)
    return pl.pallas_call(
        body,
        grid_spec=pltpu.PrefetchScalarGridSpec(
            num_scalar_prefetch=1,
            grid=(nb, n_steps + 1),
            in_specs=in_specs,
            out_specs=pl.BlockSpec((dec, d), lambda b, j, pt_ref: (b, 0)),
            scratch_shapes=[
                pltpu.VMEM((nr, d), BF16),
                pltpu.VMEM((ps, d), F32),
                pltpu.VMEM((ps, d), F32),
                pltpu.VMEM((nr, d), F32),
                pltpu.VMEM((nr, 1), F32),
            ],
        ),
        out_shape=jax.ShapeDtypeStruct((nb * dec, d), BF16),
        compiler_params=pltpu.CompilerParams(
            dimension_semantics=("arbitrary", "arbitrary"), vmem_limit_bytes=VMEM_LIMIT),
        name="attn_sample",
    )(pt, q, k, v, bias_col, *([ck] * pps), *([cv] * pps))


def _router_body(*refs, n_add, scale_lanes, n_groups, per_group):
    it = iter(refs)
    add_refs = [next(it) for _ in range(n_add)]
    sc_ref = next(it) if any(l is not None for l in scale_lanes) else None
    nw_ref = next(it)
    wr_ref = next(it)
    x_ref, hn_ref, idx_ref, wt_ref = next(it), next(it), next(it), next(it)

    x = None
    for r, ln in zip(add_refs, scale_lanes):
        v = r[...]
        if ln is not None:
            v = v * sc_ref[:, ln:ln + 1]
        x = v if x is None else x + v
    x_ref[...] = x
    ms = jnp.mean(x * x, axis=-1, keepdims=True)
    hn = (x * lax.rsqrt(ms + RMS_EPS)) * nw_ref[...]
    hn_ref[...] = hn
    logits = jnp.dot(hn, wr_ref[...], preferred_element_type=F32, precision=lax.Precision.HIGHEST)

    tm = logits.shape[0]
    lane = lax.broadcasted_iota(I32, (tm, LANES), 1)
    far = jnp.int32(4 * LANES)
    gl = jnp.where(lane < n_groups, logits, NEG_BIG)
    gmax = jnp.max(gl, axis=1, keepdims=True)
    gidx = jnp.min(jnp.where(gl == gmax, lane, far), axis=1, keepdims=True)
    gsum = jnp.sum(jnp.where(lane < n_groups, jnp.exp(logits - gmax), 0.0), axis=1, keepdims=True)
    g_w = 1.0 / gsum
    lo = n_groups + gidx * per_group
    el = jnp.where(lane >= lo, jnp.where(lane < lo + per_group, logits, NEG_BIG), NEG_BIG)
    v1 = jnp.max(el, axis=1, keepdims=True)
    i1 = jnp.min(jnp.where(el == v1, lane, far), axis=1, keepdims=True)
    el2 = jnp.where(lane == i1, NEG_BIG, el)
    v2 = jnp.max(el2, axis=1, keepdims=True)
    i2 = jnp.min(jnp.where(el2 == v2, lane, far), axis=1, keepdims=True)
    e2 = jnp.exp(v2 - v1)
    den = 1.0 + e2
    w1 = (1.0 / den) * g_w
    w2 = (e2 / den) * g_w
    idx_ref[...] = jnp.where(lane == 0, i1 - n_groups, jnp.where(lane == 1, i2 - n_groups, 0))
    wt_ref[...] = jnp.where(lane == 0, w1, jnp.where(lane == 1, w2, 0.0))


def moe_router(addends, *, rows, scale, scale_lanes, norm_w, w_route, n_groups, per_group, tm=256):
    n_add = len(addends)
    d = addends[0][0].shape[1]
    in_specs, args = [], []
    for arr, row0 in addends:
        in_specs.append(pl.BlockSpec((tm, d), functools.partial(lambda i, b: (i + b, 0), b=row0 // tm)))
        args.append(arr)
    if scale is not None:
        in_specs.append(pl.BlockSpec((tm, LANES), lambda i: (i, 0)))
        args.append(scale)
    in_specs.append(pl.BlockSpec((1, d), lambda i: (0, 0)))
    args.append(norm_w.reshape(1, d).astype(F32))
    in_specs.append(pl.BlockSpec(w_route.shape, lambda i: (0, 0)))
    args.append(w_route)
    row_spec = pl.BlockSpec((tm, d), lambda i: (i, 0))
    lane_spec = pl.BlockSpec((tm, LANES), lambda i: (i, 0))
    body = functools.partial(_router_body, n_add=n_add, scale_lanes=tuple(scale_lanes),
                             n_groups=n_groups, per_group=per_group)
    return pl.pallas_call(
        body,
        grid=(rows // tm,),
        in_specs=in_specs,
        out_specs=[row_spec, row_spec, lane_spec, lane_spec],
        out_shape=[
            jax.ShapeDtypeStruct((rows, d), F32),
            jax.ShapeDtypeStruct((rows, d), F32),
            jax.ShapeDtypeStruct((rows, LANES), I32),
            jax.ShapeDtypeStruct((rows, LANES), F32),
        ],
        compiler_params=pltpu.CompilerParams(
            dimension_semantics=("arbitrary",), vmem_limit_bytes=VMEM_LIMIT),
        name="moe_router",
    )(*args)


def _experts_body(te_ref, nv_ref, cnt_ref, src_ref, dst_ref, hn_ref, w1_ref, w3_ref, w2_ref, y_ref,
                  xbuf, ybuf, gsem, ssem):
    i = pl.program_id(0)
    nt = pl.num_programs(0)
    tm = xbuf.shape[1]
    n_valid = nv_ref[0]
    slot = i % 2

    def start_gather(tile, s):
        def issue(r, carry):
            tok = src_ref[tile * tm + r]
            pltpu.make_async_copy(hn_ref.at[pl.ds(tok, 1)], xbuf.at[s, pl.ds(r, 1)], gsem.at[s]).start()
            return carry
        lax.fori_loop(0, tm, issue, 0, unroll=8)

    def wait_rows(sem, s, buf, n):
        def wait_full():
            pltpu.make_async_copy(hn_ref.at[pl.ds(0, tm)], buf.at[s], sem.at[s]).wait()

        if isinstance(n, int):
            assert n == tm
            wait_full()
            return
        pl.when(n == tm)(wait_full)

        @pl.when(n < tm)
        def _():
            def one(r, carry):
                pltpu.make_async_copy(hn_ref.at[pl.ds(0, 1)], buf.at[s, pl.ds(0, 1)], sem.at[s]).wait()
                return carry
            lax.fori_loop(0, n, one, 0)

    @pl.when(i == 0)
    def _():
        start_gather(0, 0)

    @pl.when(i + 1 < n_valid)
    def _():
        start_gather(i + 1, 1 - slot)

    @pl.when(i < n_valid)
    def _():
        wait_rows(gsem, slot, xbuf, tm)
        x = xbuf[slot].astype(BF16)
        a = _dot(x, w1_ref[0])
        u = _dot(x, w3_ref[0])
        hid = (_silu(a) * u).astype(BF16)
        y = _dot(hid, w2_ref[0])

        @pl.when(i >= 2)
        def _():
            wait_rows(ssem, slot, ybuf, cnt_ref[jnp.maximum(i - 2, 0)])

        ybuf[slot] = y

        def issue(r, carry):
            row = dst_ref[i * tm + r]
            pltpu.make_async_copy(ybuf.at[slot, pl.ds(r, 1)], y_ref.at[pl.ds(row, 1)], ssem.at[slot]).start()
            return carry

        @pl.when(cnt_ref[i] == tm)
        def _():
            lax.fori_loop(0, tm, issue, 0, unroll=8)

        @pl.when(cnt_ref[i] < tm)
        def _():
            lax.fori_loop(0, cnt_ref[i], issue, 0)

    @pl.when(i == nt - 1)
    def _():
        wait_rows(ssem, (n_valid - 1) % 2, ybuf, cnt_ref[n_valid - 1])

        @pl.when(n_valid >= 2)
        def _():
            wait_rows(ssem, n_valid % 2, ybuf, cnt_ref[jnp.maximum(n_valid - 2, 0)])


def moe_experts(hn, tile_expert, n_valid, tile_rows, src_tok, dst_row, w1, w3, w2, *, out_rows):
    n_tiles = tile_expert.shape[0]
    d = hn.shape[1]
    tm = MOE_TM
    wmap = lambda i, te, nv, cnt, s, dd: (te[i], 0, 0)
    return pl.pallas_call(
        _experts_body,
        grid_spec=pltpu.PrefetchScalarGridSpec(
            num_scalar_prefetch=5,
            grid=(n_tiles,),
            in_specs=[
                pl.BlockSpec(memory_space=pl.ANY),
                pl.BlockSpec((1,) + w1.shape[1:], wmap),
                pl.BlockSpec((1,) + w3.shape[1:], wmap),
                pl.BlockSpec((1,) + w2.shape[1:], wmap),
            ],
            out_specs=pl.BlockSpec(memory_space=pl.ANY),
            scratch_shapes=[
                pltpu.VMEM((2, tm, d), F32),
                pltpu.VMEM((2, tm, d), F32),
                pltpu.SemaphoreType.DMA((2,)),
                pltpu.SemaphoreType.DMA((2,)),
            ],
        ),
        out_shape=jax.ShapeDtypeStruct((out_rows, d), F32),
        compiler_params=pltpu.CompilerParams(
            dimension_semantics=("arbitrary",), vmem_limit_bytes=VMEM_LIMIT),
        name="moe_experts",
    )(tile_expert, n_valid, tile_rows, src_tok, dst_row, hn, w1, w3, w2)


def _dispatch_plan(eid, n_experts, n_tiles):
    t = eid.shape[0]
    tm = MOE_TM
    e_flat = eid.reshape(-1)
    onehot = (e_flat[:, None] == jnp.arange(n_experts, dtype=I32)[None, :]).astype(I32)
    rank = jnp.sum((jnp.cumsum(onehot, axis=0) - onehot) * onehot, axis=1)
    counts = jnp.sum(onehot, axis=0)
    padded = (counts + tm - 1) // tm * tm
    ends = jnp.cumsum(padded)
    offs = ends - padded
    pos = offs[e_flat] + rank
    n_sorted = n_tiles * tm
    tok = jnp.arange(t * TOP_K, dtype=I32) // TOP_K
    slot = jnp.arange(t * TOP_K, dtype=I32) % TOP_K
    src_tok = jnp.zeros((n_sorted,), I32).at[pos].set(tok)
    dst_row = jnp.zeros((n_sorted,), I32).at[pos].set(slot * t + tok)
    n_valid = (ends[-1] // tm).astype(I32)
    tile_start = jnp.arange(n_tiles, dtype=I32) * tm
    tile_expert = jnp.sum((tile_start[:, None] >= ends[None, :]).astype(I32), axis=1)
    last_e = jnp.sum((jnp.maximum(ends[-1] - tm, 0) >= ends).astype(I32))
    tile_expert = jnp.minimum(tile_expert, last_e).astype(I32)
    tile_rows = jnp.clip(counts[tile_expert] - (tile_start - offs[tile_expert]), 0, tm)
    tile_rows = jnp.where(tile_start < ends[-1], tile_rows, 0).astype(I32)
    return tile_expert, n_valid.reshape(1), tile_rows, src_tok, dst_row


def moe_layer(addends, scale, scale_lanes, norm_w, w_group, w_expert, w1, w3, w2, *, rows):
    d = w_group.shape[0]
    n_groups = w_group.shape[1]
    n_experts = w_expert.shape[1]
    w_route = jnp.zeros((d, LANES), F32).at[:, :n_groups].set(w_group).at[:, n_groups:n_groups + n_experts].set(w_expert)
    x, hn, idx, wts = moe_router(addends, rows=rows, scale=scale, scale_lanes=scale_lanes, norm_w=norm_w,
                                 w_route=w_route, n_groups=n_groups, per_group=n_experts // n_groups)
    n_tiles = (rows * TOP_K + n_experts * (MOE_TM - 1) + MOE_TM - 1) // MOE_TM
    plan = _dispatch_plan(idx[:, :TOP_K], n_experts, n_tiles)
    y2 = moe_experts(hn, *plan, w1.astype(BF16), w3.astype(BF16), w2.astype(BF16),
                     out_rows=TOP_K * rows)
    return x, wts, y2


def kernel(x_prompt, x_sample, state_ssm, state_conv, cache_k, cache_v, page_table, meta_tokens,
           norm_mix, norm_ffn, norm_final, m_w_in, m_conv_w, m_conv_b, m_dt_bias, m_A_log, m_D,
           m_norm_w, m_w_out, a_w_qkv, a_w_o, a_logit_bias, moe_w_group, moe_w_expert, moe_w1, moe_w3, moe_w2):
    bp, seq, d = x_prompt.shape
    nb, dec, _ = x_sample.shape
    n_meta = meta_tokens.shape[0]
    assert bp == 1 and dec == SUBLANES
    d_inner = m_w_out.shape[0]
    n_heads_ssm = m_A_log.shape[0]
    conv_dim = m_conv_w.shape[1]
    d_state = state_ssm.shape[-1]
    hd_ssm = d_inner // n_heads_ssm
    n_heads = a_logit_bias.shape[0]
    hd = d // n_heads

    p_len = n_meta + seq
    n_pad = (-p_len) % SSD_CHUNK
    tp = n_pad + p_len
    tq = _round_up(tp, ATTN_TQ)
    ns = nb * dec
    row_s = tq
    rows = _round_up(row_s + ns, ROW_TILE)

    x0 = jnp.concatenate([
        jnp.zeros((n_pad, d), F32), meta_tokens.astype(F32), x_prompt[0],
        jnp.zeros((row_s - tp, d), F32), x_sample.reshape(ns, d),
        jnp.zeros((rows - row_s - ns, d), F32)], axis=0)

    w_in = m_w_in.astype(BF16)
    w_z = w_in[:, :d_inner]
    w_xbc = w_in[:, d_inner:d_inner + conv_dim]
    w_dt = jnp.zeros((d, LANES), BF16).at[:, :n_heads_ssm].set(w_in[:, d_inner + conv_dim:])
    z, xbc, dtr = fused_linear([(x0, 0)], rows=rows, norm_w=norm_mix[0], weights=(w_z, w_xbc, w_dt),
                               out_dtypes=((F32,), (F32,), (F32,)), name="in_proj")

    head_of_lane = jnp.arange(d_inner, dtype=I32) // hd_ssm
    e_sel = (jnp.arange(LANES, dtype=I32)[:, None] == head_of_lane[None, :]).astype(BF16)
    pad_h = LANES - n_heads_ssm
    ssd_params = dict(
        n_heads=n_heads_ssm,
        conv_w=m_conv_w.astype(F32), conv_b=m_conv_b.reshape(1, conv_dim).astype(F32),
        dt_bias=jnp.pad(m_dt_bias.astype(F32), (0, pad_h)).reshape(1, LANES),
        a_log=jnp.pad(m_A_log.astype(F32), (0, pad_h)).reshape(1, LANES),
        d_skip=jnp.repeat(m_D.astype(F32), hd_ssm).reshape(1, d_inner),
        norm_w=m_norm_w.reshape(1, d_inner).astype(F32),
        e_sel=e_sel, e_sel_t=e_sel.T)
    g_p, ssm_p = ssd_scan(xbc[None], dtr[None], z[None], jnp.zeros((1, d_inner, d_state), F32),
                          ssd_params, n_rows=tp, n_first=n_pad, name="ssd_prompt")

    L = SSD_CHUNK
    kw = state_conv.shape[1]
    lead = L - dec - kw

    def seq_chunk(a, pre):
        a = a[row_s:row_s + ns].reshape(nb, dec, a.shape[1])
        return jnp.concatenate([jnp.zeros((nb, lead, a.shape[2]), F32), pre, a], axis=1)

    xbc_s = seq_chunk(xbc, state_conv.astype(F32))
    dt_s = seq_chunk(dtr, jnp.zeros((nb, kw, LANES), F32))
    z_s = seq_chunk(z, jnp.zeros((nb, kw, d_inner), F32))
    g_s, ssm_s = ssd_scan(xbc_s, dt_s, z_s, state_ssm.reshape(nb, d_inner, d_state).astype(F32),
                          ssd_params, n_rows=L, n_first=L - dec, name="ssd_sample")

    g = jnp.concatenate([g_p[0], jnp.zeros((row_s - tp, d_inner), BF16), g_s[:, L - dec:].reshape(ns, d_inner),
                         jnp.zeros((rows - row_s - ns, d_inner), BF16)], axis=0)
    (x1,) = fused_linear([(g, 0)], rows=rows, weights=(m_w_out.astype(BF16),), out_dtypes=((F32,),),
                         residual=x0, name="out_proj")

    conv_p = xbc[tp - kw:tp][None]
    conv_s = xbc[row_s:row_s + ns].reshape(nb, dec, conv_dim)[:, dec - kw:]

    x1, wts0, y0 = moe_layer([(x1, 0)], None, (None,), norm_ffn[0], moe_w_group[0], moe_w_expert[0],
                             moe_w1[0], moe_w3[0], moe_w2[0], rows=rows)

    qscale = (hd ** -0.5) * LOG2E
    w_q = (a_w_qkv[:, :d] * qscale).astype(BF16)
    w_k = a_w_qkv[:, d:2 * d].astype(BF16)
    w_v = a_w_qkv[:, 2 * d:].astype(BF16)
    x2, qf, qb, kf, kb, vf, vb = fused_linear(
        [(x1, 0), (y0, 0), (y0, rows)], rows=rows, scale=wts0, scale_lanes=(None, 0, 1), norm_w=norm_mix[1],
        weights=(w_q, w_k, w_v), out_dtypes=((F32, BF16), (F32, BF16), (F32, BF16)), emit_x=True, name="qkv_proj")
    bias2 = a_logit_bias.astype(F32) * LOG2E
    aux_w = LANES - hd
    b_hi = bias2.astype(BF16)
    b_lo = (bias2 - b_hi.astype(F32)).astype(BF16)
    aux_q = jnp.zeros((n_heads, aux_w), BF16).at[:, 0].set(b_hi).at[:, 1].set(b_lo).at[:, 2].set(1.0)
    q_aug = jnp.concatenate([qb.reshape(rows, n_heads, hd),
                             jnp.broadcast_to(aux_q[None], (rows, n_heads, aux_w))], axis=2)
    pad_mask = jnp.where(jnp.arange(rows) < n_pad, MASKED_LOGIT, 0.0).astype(BF16)
    aux_k = jnp.concatenate([jnp.ones((rows, n_heads, 2), BF16),
                             jnp.broadcast_to(pad_mask[:, None, None], (rows, n_heads, 1)),
                             jnp.zeros((rows, n_heads, aux_w - 3), BF16)], axis=2)
    k_aug = jnp.concatenate([kb.reshape(rows, n_heads, hd), aux_k], axis=2)
    o_p = attn_prompt(q_aug.reshape(rows, n_heads * LANES), k_aug.reshape(rows, n_heads * LANES), vb,
                      rows=tq, hd=hd)
    bias_col = jnp.repeat(bias2, dec).reshape(n_heads * dec, 1)
    o_s = attn_sample(qf, kf, vf, row_s, cache_k, cache_v, page_table, bias_col, n_heads=n_heads, hd=hd)
    o = jnp.concatenate([o_p, o_s, jnp.zeros((rows - row_s - ns, d), BF16)], axis=0)
    (x3,) = fused_linear([(o, 0)], rows=rows, weights=(a_w_o.astype(BF16),), out_dtypes=((F32,),),
                         residual=x2, name="attn_out_proj")

    x3, wts1, y1 = moe_layer([(x3, 0)], None, (None,), norm_ffn[1], moe_w_group[1], moe_w_expert[1],
                             moe_w1[1], moe_w3[1], moe_w2[1], rows=rows)

    (yn,) = fused_linear([(x3, 0), (y1, 0), (y1, rows)], rows=rows, scale=wts1, scale_lanes=(None, 0, 1),
                         norm_w=norm_final, emit_norm=True, name="final_norm")

    y_prompt = yn[n_pad + n_meta:tp][None]
    y_sample = yn[row_s:row_s + ns].reshape(nb, dec, d)
    k_p = kf[n_pad:tp].reshape(1, p_len, n_heads, hd)
    v_p = vf[n_pad:tp].reshape(1, p_len, n_heads, hd)
    k_s = kf[row_s:row_s + ns].reshape(nb, dec, n_heads, hd)
    v_s = vf[row_s:row_s + ns].reshape(nb, dec, n_heads, hd)
    return (y_prompt, y_sample, ssm_p.reshape(1, n_heads_ssm, hd_ssm, d_state), conv_p,
            k_p, v_p, ssm_s.reshape(nb, n_heads_ssm, hd_ssm, d_state), conv_s, k_s, v_s)
```

```python
import functools
import math

import jax
import jax.numpy as jnp
from jax import lax
from jax.experimental import pallas as pl
from jax.experimental.pallas import tpu as pltpu

F32 = jnp.float32
BF16 = jnp.bfloat16
I32 = jnp.int32

RMS_EPS = 1e-6
LOG2E = 1.4426950408889634
NEG_BIG = -3.0e38

LANES = 128
SUBLANES = 8
VMEM_LIMIT = 48 * 1024 * 1024

SSD_CHUNK = 128
ROW_TILE = 512
ATTN_TQ = 256
ATTN_TK = 128
ATTN_HEADS_PER_STEP = 4
MOE_TM = 256
TOP_K = 2
PAGES_PER_STEP = 8


def _round_up(x, m):
    return (x + m - 1) // m * m


def _dot(a, b):
    return jnp.dot(a, b, preferred_element_type=F32)


def _dot_nt(a, b):
    return lax.dot_general(a, b, (((1,), (1,)), ((), ())), preferred_element_type=F32)


def _split3(x):
    hi = x.astype(BF16)
    r1 = x - hi.astype(F32)
    mid = r1.astype(BF16)
    lo = (r1 - mid.astype(F32)).astype(BF16)
    return hi, mid, lo


def _dot_exact_rhs(sel, x):
    hi, mid, lo = _split3(x)
    return _dot(sel, hi) + _dot(sel, mid) + _dot(sel, lo)


def _dot_exact_lhs(x, sel):
    hi, mid, lo = _split3(x)
    return _dot(hi, sel) + _dot(mid, sel) + _dot(lo, sel)


def _silu(x):
    return x / (1.0 + jnp.exp(-x))


def _softplus(x):
    return jnp.maximum(x, 0.0) + jnp.log1p(jnp.exp(-jnp.abs(x)))


def _linear_body(*refs, n_add, scale_lanes, has_scale, has_norm, n_w, has_res,
                 emit_x, emit_norm, n_outs, tn):
    it = iter(refs)
    add_refs = [next(it) for _ in range(n_add)]
    sc_ref = next(it) if has_scale else None
    nw_ref = next(it) if has_norm else None
    w_refs = [next(it) for _ in range(n_w)]
    res_ref = next(it) if has_res else None
    xo_ref = next(it) if emit_x else None
    no_ref = next(it) if emit_norm else None
    out_refs = [[next(it) for _ in range(k)] for k in n_outs]
    xb_ref = next(it) if n_w else None

    x = None
    for r, lane in zip(add_refs, scale_lanes):
        v = r[...]
        if lane is not None:
            v = v * sc_ref[:, lane:lane + 1]
        x = v if x is None else x + v
    if emit_x:
        xo_ref[...] = x
    if has_norm:
        ms = jnp.mean(x * x, axis=-1, keepdims=True)
        x = (x * lax.rsqrt(ms + RMS_EPS)) * nw_ref[...]
        if emit_norm:
            no_ref[...] = x
    if not n_w:
        return
    xb_ref[...] = x.astype(BF16)
    for k, w_ref in enumerate(w_refs):
        n_cols = w_ref.shape[1]
        for n0 in range(0, n_cols, tn):
            n1 = min(n0 + tn, n_cols)
            r = _dot(xb_ref[...], w_ref[:, n0:n1])
            if has_res and k == 0:
                r = r + res_ref[:, n0:n1]
            for o_ref in out_refs[k]:
                o_ref[:, n0:n1] = r.astype(o_ref.dtype)


def fused_linear(addends, *, rows, scale=None, scale_lanes=None, norm_w=None, weights=(),
                 out_dtypes=(), residual=None, emit_x=False, emit_norm=False,
                 tm=256, tn=512, name="fused_linear"):
    n_add = len(addends)
    if scale_lanes is None:
        scale_lanes = (None,) * n_add
    d_in = addends[0][0].shape[1]
    assert rows % tm == 0
    in_specs, args = [], []
    for arr, row0 in addends:
        assert row0 % tm == 0
        in_specs.append(pl.BlockSpec((tm, d_in), functools.partial(lambda i, b: (i + b, 0), b=row0 // tm)))
        args.append(arr)
    if scale is not None:
        in_specs.append(pl.BlockSpec((tm, LANES), lambda i: (i, 0)))
        args.append(scale)
    if norm_w is not None:
        in_specs.append(pl.BlockSpec((1, d_in), lambda i: (0, 0)))
        args.append(norm_w.reshape(1, d_in).astype(F32))
    for w in weights:
        in_specs.append(pl.BlockSpec(w.shape, lambda i: (0, 0)))
        args.append(w)
    if residual is not None:
        in_specs.append(pl.BlockSpec((tm, residual.shape[1]), lambda i: (i, 0)))
        args.append(residual)
    out_shapes, out_specs = [], []
    if emit_x:
        out_shapes.append(jax.ShapeDtypeStruct((rows, d_in), F32))
        out_specs.append(pl.BlockSpec((tm, d_in), lambda i: (i, 0)))
    if emit_norm:
        out_shapes.append(jax.ShapeDtypeStruct((rows, d_in), F32))
        out_specs.append(pl.BlockSpec((tm, d_in), lambda i: (i, 0)))
    for w, dts in zip(weights, out_dtypes):
        for dt in dts:
            out_shapes.append(jax.ShapeDtypeStruct((rows, w.shape[1]), dt))
            out_specs.append(pl.BlockSpec((tm, w.shape[1]), lambda i: (i, 0)))
    scratch = [pltpu.VMEM((tm, d_in), BF16)] if weights else []
    body = functools.partial(
        _linear_body, n_add=n_add, scale_lanes=tuple(scale_lanes), has_scale=scale is not None,
        has_norm=norm_w is not None, n_w=len(weights), has_res=residual is not None,
        emit_x=emit_x, emit_norm=emit_norm, n_outs=tuple(len(d) for d in out_dtypes), tn=tn)
    return pl.pallas_call(
        body,
        grid=(rows // tm,),
        in_specs=in_specs,
        out_specs=out_specs,
        out_shape=out_shapes,
        scratch_shapes=scratch,
        compiler_params=pltpu.CompilerParams(
            dimension_semantics=("arbitrary",), vmem_limit_bytes=VMEM_LIMIT),
        name=name,
    )(*args)


def _ssd_body(xbc_ref, dt_ref, z_ref, s0_ref, cw_ref, cb_ref, dtb_ref, alog_ref, dsk_ref,
              nw_ref, e_ref, et_ref, g_ref, sout_ref, ext_ref, s_ref, y_ref, *,
              n_first, d_inner, n_groups, d_state, n_heads):
    c = pl.program_id(1)
    L = SSD_CHUNK
    hd = d_inner // n_heads
    gw = n_groups * d_state
    hpg = n_heads // n_groups

    @pl.when(c == 0)
    def _():
        ext_ref[0:SUBLANES, :] = jnp.zeros((SUBLANES, ext_ref.shape[1]), F32)
        s_ref[...] = s0_ref[0]

    ext_ref[SUBLANES:SUBLANES + L, :] = xbc_ref[0]
    conv = cb_ref[...] + ext_ref[5:5 + L, :] * cw_ref[0:1, :]
    conv = conv + ext_ref[6:6 + L, :] * cw_ref[1:2, :]
    conv = conv + ext_ref[7:7 + L, :] * cw_ref[2:3, :]
    conv = conv + ext_ref[8:8 + L, :] * cw_ref[3:4, :]
    ext_ref[0:SUBLANES, :] = ext_ref[L:L + SUBLANES, :]

    row = lax.broadcasted_iota(I32, (L, 1), 0)
    n0 = jnp.where(c == 0, n_first, 0)
    valid = (row >= n0).astype(F32)
    xbc = _silu(conv) * valid
    xs = xbc[:, :d_inner]

    dt = _softplus(dt_ref[0] + dtb_ref[...]) * valid
    a = dt * (-jnp.exp(alog_ref[...]))
    ri = lax.broadcasted_iota(I32, (L, L), 0)
    ci = lax.broadcasted_iota(I32, (L, L), 1)
    tri = ri >= ci
    acs = _dot_exact_rhs(tri.astype(BF16), a)
    acs_t = acs.T
    acs_last = acs[L - 1:L, :]

    e_sel = e_ref[...]
    dt_e = _dot_exact_lhs(dt, e_sel)
    grow_e = _dot_exact_lhs(jnp.exp(acs), e_sel)
    dec_e = _dot_exact_lhs(jnp.exp(acs_last - acs), e_sel)
    xdt = xs * dt_e
    xw = xdt * dec_e

    lane = lax.broadcasted_iota(I32, (L, LANES), 1)
    lo_half = lane < hd
    s_new = []
    for g in range(n_groups):
        bg = xbc[:, d_inner + g * d_state:d_inner + (g + 1) * d_state].astype(BF16)
        cg = xbc[:, d_inner + gw + g * d_state:d_inner + gw + (g + 1) * d_state].astype(BF16)
        cb = _dot_nt(cg, bg)
        for j in range(hpg // 2):
            h0 = g * hpg + 2 * j
            ms = []
            for h in (h0, h0 + 1):
                diff = acs[:, h:h + 1] - acs_t[h:h + 1, :]
                ms.append((cb * jnp.where(tri, jnp.exp(diff), 0.0)).astype(BF16))
            lhs = jnp.concatenate(ms, axis=1)
            xp = xdt[:, h0 * hd:(h0 + 2) * hd]
            rhs = jnp.concatenate([jnp.where(lo_half, xp, 0.0), jnp.where(lo_half, 0.0, xp)],
                                  axis=0).astype(BF16)
            y_ref[:, h0 * hd:(h0 + 2) * hd] = _dot(lhs, rhs)
        c0, c1 = g * hpg * hd, (g + 1) * hpg * hd
        s_g = s_ref[c0:c1, :]
        y_off = _dot_nt(cg, s_g.astype(BF16)) * grow_e[:, c0:c1]
        y_ref[:, c0:c1] = y_ref[:, c0:c1] + y_off
        s_new.append(_dot(xw[:, c0:c1].T.astype(BF16), bg))

    last_col = jnp.broadcast_to(acs_t[:, L - 1:L], (LANES, LANES))
    cdec = jnp.exp(_dot_exact_rhs(et_ref[...], last_col))
    s_ref[...] = s_ref[...] * cdec + jnp.concatenate(s_new, axis=0)

    y = y_ref[...] + xs * dsk_ref[...]
    gt = y * _silu(z_ref[0])
    gsz = d_inner // n_groups
    outs = []
    for g in range(n_groups):
        gg = gt[:, g * gsz:(g + 1) * gsz]
        ms = jnp.mean(gg * gg, axis=-1, keepdims=True)
        outs.append(gg * lax.rsqrt(ms + RMS_EPS))
    g_ref[0] = (jnp.concatenate(outs, axis=1) * nw_ref[...]).astype(g_ref.dtype)

    @pl.when(c == pl.num_programs(1) - 1)
    def _():
        sout_ref[0] = s_ref[...]


def ssd_scan(xbc, dt, z, s0, params, *, n_rows, n_first, name):
    nb, _, conv_dim = xbc.shape
    t = n_rows
    d_inner = z.shape[2]
    d_state = s0.shape[2]
    n_heads = params["n_heads"]
    n_groups = (conv_dim - d_inner) // (2 * d_state)
    assert t % SSD_CHUNK == 0 and (d_inner // n_heads) * 2 == LANES
    nc = t // SSD_CHUNK
    L = SSD_CHUNK
    const = lambda b, c: (0, 0)
    body = functools.partial(_ssd_body, n_first=n_first, d_inner=d_inner, n_groups=n_groups,
                             d_state=d_state, n_heads=n_heads)
    return pl.pallas_call(
        body,
        grid=(nb, nc),
        in_specs=[
            pl.BlockSpec((1, L, conv_dim), lambda b, c: (b, c, 0)),
            pl.BlockSpec((1, L, LANES), lambda b, c: (b, c, 0)),
            pl.BlockSpec((1, L, d_inner), lambda b, c: (b, c, 0)),
            pl.BlockSpec((1, d_inner, d_state), lambda b, c: (b, 0, 0)),
            pl.BlockSpec(params["conv_w"].shape, const),
            pl.BlockSpec(params["conv_b"].shape, const),
            pl.BlockSpec(params["dt_bias"].shape, const),
            pl.BlockSpec(params["a_log"].shape, const),
            pl.BlockSpec(params["d_skip"].shape, const),
            pl.BlockSpec(params["norm_w"].shape, const),
            pl.BlockSpec(params["e_sel"].shape, const),
            pl.BlockSpec(params["e_sel_t"].shape, const),
        ],
        out_specs=[
            pl.BlockSpec((1, L, d_inner), lambda b, c: (b, c, 0)),
            pl.BlockSpec((1, d_inner, d_state), lambda b, c: (b, 0, 0)),
        ],
        out_shape=[
            jax.ShapeDtypeStruct((nb, t, d_inner), BF16),
            jax.ShapeDtypeStruct((nb, d_inner, d_state), F32),
        ],
        scratch_shapes=[
            pltpu.VMEM((L + SUBLANES, conv_dim), F32),
            pltpu.VMEM((d_inner, d_state), F32),
            pltpu.VMEM((L, d_inner), F32),
        ],
        compiler_params=pltpu.CompilerParams(
            dimension_semantics=("arbitrary", "arbitrary"), vmem_limit_bytes=VMEM_LIMIT),
        name=name,
    )(xbc, dt, z, s0, params["conv_w"], params["conv_b"], params["dt_bias"], params["a_log"],
      params["d_skip"], params["norm_w"], params["e_sel"], params["e_sel_t"])


MASKED_LOGIT = -1e30


def _suffix_sum_matrix(tk):
    kr = lax.broadcasted_iota(I32, (tk, tk), 0)
    kc = lax.broadcasted_iota(I32, (tk, tk), 1)
    return jnp.where(kr >= kc, -1.0, 0.0).astype(BF16)


def _sb_stage_a(z, vis):
    neg_abs = pltpu.bitcast(pltpu.bitcast(z, jnp.uint32) | jnp.uint32(0x80000000), F32)
    sp = jnp.maximum(z, 0.0) + jnp.log2(1.0 + jnp.exp2(neg_abs))
    if vis is not None:
        sp = jnp.where(vis, sp, 0.0)
        z = jnp.where(vis, z, MASKED_LOGIT)
    return z, sp.astype(BF16)


def _sb_stage_b(z, sp_split, carry, u2neg):
    suffix = _dot(sp_split, u2neg)
    return jnp.exp2(z + suffix + carry), carry + suffix[:, 0:1]


def _attn_prompt_body(q_ref, k_ref, v_ref, o_ref, z_scr, sp_scr, *, hd):
    qi = pl.program_id(1)
    tq, tk = ATTN_TQ, ATTN_TK
    nh = ATTN_HEADS_PER_STEP
    u2neg = _suffix_sum_matrix(tk)
    lo_v = lax.broadcasted_iota(I32, (tk, LANES), 1) < hd
    qrow = qi * tq + lax.broadcasted_iota(I32, (tq, tk), 0)
    kidx = lax.broadcasted_iota(I32, (tq, tk), 1)

    def stage_a(j, kb, masked):
        k0 = pl.multiple_of(kb * tk, tk)
        vis = (k0 + kidx) < qrow if masked else None
        for hh in range(nh):
            cols = slice(hh * LANES, (hh + 1) * LANES)
            z, sp = _sb_stage_a(_dot_nt(q_ref[:, cols], k_ref[pl.ds(k0, tk), cols]), vis)
            z_scr[j, hh] = z
            sp_scr[j, hh] = sp

    def stage_b(j, kb, carries, accs):
        k0 = pl.multiple_of(kb * tk, tk)
        carries, accs = list(carries), list(accs)
        for p in range(nh // 2):
            vblk = v_ref[pl.ds(k0, tk), p * LANES:(p + 1) * LANES]
            ws = []
            for hh in (2 * p, 2 * p + 1):
                w, carries[hh] = _sb_stage_b(z_scr[j, hh], sp_scr[j, hh], carries[hh], u2neg)
                ws.append(w.astype(BF16))
            vcat = jnp.concatenate([jnp.where(lo_v, vblk, 0), jnp.where(lo_v, 0, vblk)], axis=0)
            accs[p] = accs[p] + _dot(jnp.concatenate(ws, axis=1), vcat)
        return tuple(carries), tuple(accs)

    nd = tq // tk
    n_full = qi * nd
    carries = tuple(jnp.zeros((tq, 1), F32) for _ in range(nh))
    accs = tuple(jnp.zeros((tq, LANES), F32) for _ in range(nh // 2))
    for j in range(nd):
        stage_a(j, n_full + nd - 1 - j, True)

    def step(t, state):
        carries, accs = state
        kb0 = n_full - 1 - nd * t
        for j in range(nd):
            carries, accs = stage_b(j, kb0 + nd - j, carries, accs)
        for j in range(nd):
            stage_a(j, kb0 - j, False)
        return carries, accs

    carries, accs = lax.fori_loop(0, qi, step, (carries, accs))
    for j in range(nd):
        carries, accs = stage_b(j, nd - 1 - j, carries, accs)
    for p in range(nh // 2):
        o_ref[:, p * LANES:(p + 1) * LANES] = accs[p].astype(o_ref.dtype)


def attn_prompt(q_aug, k_aug, vb, *, rows, hd):
    d = vb.shape[1]
    nh = ATTN_HEADS_PER_STEP
    n_heads = d // hd
    assert ATTN_TQ % ATTN_TK == 0 and n_heads % nh == 0 and nh % 2 == 0 and 2 * hd == LANES
    body = functools.partial(_attn_prompt_body, hd=hd)
    resident = dict(pipeline_mode=pl.Buffered(1))
    return pl.pallas_call(
        body,
        grid=(n_heads // nh, rows // ATTN_TQ),
        in_specs=[
            pl.BlockSpec((ATTN_TQ, nh * LANES), lambda g, i: (i, g)),
            pl.BlockSpec((rows, nh * LANES), lambda g, i: (0, g), **resident),
            pl.BlockSpec((rows, nh * hd), lambda g, i: (0, g), **resident),
        ],
        out_specs=pl.BlockSpec((ATTN_TQ, nh * hd), lambda g, i: (i, g)),
        out_shape=jax.ShapeDtypeStruct((rows, d), BF16),
        scratch_shapes=[
            pltpu.VMEM((ATTN_TQ // ATTN_TK, nh, ATTN_TQ, ATTN_TK), F32),
            pltpu.VMEM((ATTN_TQ // ATTN_TK, nh, ATTN_TQ, ATTN_TK), BF16),
        ],
        compiler_params=pltpu.CompilerParams(
            dimension_semantics=("arbitrary", "arbitrary"), vmem_limit_bytes=VMEM_LIMIT),
        name="attn_prompt",
    )(q_aug, k_aug, vb)


def _attn_sample_body(pt_ref, q_ref, kn_ref, vn_ref, bias_ref, *rest, n_heads, hd, pps, dec):
    k_refs = rest[:pps]
    v_refs = rest[pps:2 * pps]
    o_ref = rest[2 * pps]
    qbd_ref, kpad_ref, vpad_ref, acc_ref, carry_ref = rest[2 * pps + 1:]
    j = pl.program_id(1)
    nr = n_heads * dec
    ps = k_refs[0].shape[2]
    u2neg = _suffix_sum_matrix(ps)
    b2 = bias_ref[...]

    def process(kblk, vblk, vis):
        z = _dot(qbd_ref[...], kblk) + b2
        w, carry = _sb_stage_b(*_sb_stage_a(z, vis), carry_ref[...], u2neg)
        carry_ref[...] = carry
        acc_ref[...] += _dot_nt(w.astype(BF16), vblk)

    @pl.when(j == 0)
    def _():
        d = q_ref.shape[1]
        qt = jnp.concatenate([q_ref[...]] * n_heads, axis=0)
        rh = lax.broadcasted_iota(I32, (nr, d), 0) // dec
        lh = lax.broadcasted_iota(I32, (nr, d), 1) // hd
        qbd_ref[...] = jnp.where(rh == lh, qt, 0.0).astype(BF16)
        kpad_ref[...] = jnp.zeros(kpad_ref.shape, F32)
        vpad_ref[...] = jnp.zeros(vpad_ref.shape, F32)
        kpad_ref[0:dec, :] = kn_ref[...]
        vpad_ref[0:dec, :] = vn_ref[...]
        acc_ref[...] = jnp.zeros(acc_ref.shape, F32)
        carry_ref[...] = jnp.zeros(carry_ref.shape, F32)
        qrow = lax.broadcasted_iota(I32, (nr, ps), 0) % dec
        kidx = lax.broadcasted_iota(I32, (nr, ps), 1)
        process(kpad_ref[...].T.astype(BF16), vpad_ref[...].T.astype(BF16), kidx < qrow)

    @pl.when(j > 0)
    def _():
        order = range(pps - 1, -1, -1)
        zs = [_dot(qbd_ref[...], k_refs[i][0].astype(BF16)) + b2 for i in order]
        carry = carry_ref[...]
        ws = []
        for z in zs:
            w, carry = _sb_stage_b(*_sb_stage_a(z, None), carry, u2neg)
            ws.append(w.astype(BF16))
        carry_ref[...] = carry
        vcat = jnp.concatenate([v_refs[i][0].astype(BF16) for i in order], axis=1)
        acc_ref[...] += _dot_nt(jnp.concatenate(ws, axis=1), vcat)

    @pl.when(j == pl.num_programs(1) - 1)
    def _():
        d = o_ref.shape[1]
        lh = lax.broadcasted_iota(I32, (dec, d), 1) // hd
        out = jnp.zeros((dec, d), F32)
        for h in range(n_heads):
            out = out + jnp.where(lh == h, acc_ref[h * dec:(h + 1) * dec, :], 0.0)
        o_ref[...] = out.astype(o_ref.dtype)


def attn_sample(q, k, v, row0, cache_k, cache_v, page_table, bias_col, *, n_heads, hd):
    nb, n_pages = page_table.shape
    n_pool, ps = cache_k.shape[0], cache_k.shape[1]
    d = n_heads * hd
    dec = SUBLANES
    nr = n_heads * dec
    assert nr == ps == LANES and row0 % dec == 0
    pps = min(PAGES_PER_STEP, n_pages)
    assert n_pages % pps == 0
    n_steps = n_pages // pps
    ck = jnp.transpose(cache_k, (0, 2, 3, 1)).reshape(n_pool, d, ps)
    cv = jnp.transpose(cache_v, (0, 2, 3, 1)).reshape(n_pool, d, ps)
    pt = page_table.reshape(-1).astype(I32)
    rb0 = row0 // dec

    def page_map(b, j, pt_ref, *, i):
        jj = jnp.maximum(j, 1)
        return (pt_ref[b * n_pages + n_pages - jj * pps + i], 0, 0)

    new_map = lambda b, j, pt_ref: (rb0 + b, 0)
    in_specs = [
        pl.BlockSpec((dec, d), new_map),
        pl.BlockSpec((dec, d), new_map),
        pl.BlockSpec((dec, d), new_map),
        pl.BlockSpec((nr, 1), lambda b, j, pt_ref: (0, 0)),
    ]
    in_specs += [pl.BlockSpec((1, d, ps), functools.partial(page_map, i=i)) for i in range(pps)]
    in_specs += [pl.BlockSpec((1, d, ps), functools.partial(page_map, i=i)) for i in range(pps)]
    body = functools.partial(_attn_sample_body, n_heads=n_heads, hd=hd, pps=pps, dec=dec)
    return pl.pallas_call(
        body,
        grid_spec=pltpu.PrefetchScalarGridSpec(
            num_scalar_prefetch=1,
            grid=(nb, n_steps + 1),
            in_specs=in_specs,
            out_specs=pl.BlockSpec((dec, d), lambda b, j, pt_ref: (b, 0)),
            scratch_shapes=[
                pltpu.VMEM((nr, d), BF16),
                pltpu.VMEM((ps, d), F32),
                pltpu.VMEM((ps, d), F32),
                pltpu.VMEM((nr, d), F32),
                pltpu.VMEM((nr, 1), F32),
            ],
        ),
        out_shape=jax.ShapeDtypeStruct((nb * dec, d), BF16),
        compiler_params=pltpu.CompilerParams(
            dimension_semantics=("arbitrary", "arbitrary"), vmem_limit_bytes=VMEM_LIMIT),
        name="attn_sample",
    )(pt, q, k, v, bias_col, *([ck] * pps), *([cv] * pps))


def _router_body(*refs, n_add, scale_lanes, n_groups, per_group):
    it = iter(refs)
    add_refs = [next(it) for _ in range(n_add)]
    sc_ref = next(it) if any(l is not None for l in scale_lanes) else None
    nw_ref = next(it)
    wr_ref = next(it)
    x_ref, hn_ref, idx_ref, wt_ref = next(it), next(it), next(it), next(it)

    x = None
    for r, ln in zip(add_refs, scale_lanes):
        v = r[...]
        if ln is not None:
            v = v * sc_ref[:, ln:ln + 1]
        x = v if x is None else x + v
    x_ref[...] = x
    ms = jnp.mean(x * x, axis=-1, keepdims=True)
    hn = (x * lax.rsqrt(ms + RMS_EPS)) * nw_ref[...]
    hn_ref[...] = hn
    logits = jnp.dot(hn, wr_ref[...], preferred_element_type=F32, precision=lax.Precision.HIGHEST)

    tm = logits.shape[0]
    lane = lax.broadcasted_iota(I32, (tm, LANES), 1)
    far = jnp.int32(4 * LANES)
    gl = jnp.where(lane < n_groups, logits, NEG_BIG)
    gmax = jnp.max(gl, axis=1, keepdims=True)
    gidx = jnp.min(jnp.where(gl == gmax, lane, far), axis=1, keepdims=True)
    gsum = jnp.sum(jnp.where(lane < n_groups, jnp.exp(logits - gmax), 0.0), axis=1, keepdims=True)
    g_w = 1.0 / gsum
    lo = n_groups + gidx * per_group
    el = jnp.where(lane >= lo, jnp.where(lane < lo + per_group, logits, NEG_BIG), NEG_BIG)
    v1 = jnp.max(el, axis=1, keepdims=True)
    i1 = jnp.min(jnp.where(el == v1, lane, far), axis=1, keepdims=True)
    el2 = jnp.where(lane == i1, NEG_BIG, el)
    v2 = jnp.max(el2, axis=1, keepdims=True)
    i2 = jnp.min(jnp.where(el2 == v2, lane, far), axis=1, keepdims=True)
    e2 = jnp.exp(v2 - v1)
    den = 1.0 + e2
    w1 = (1.0 / den) * g_w
    w2 = (e2 / den) * g_w
    idx_ref[...] = jnp.where(lane == 0, i1 - n_groups, jnp.where(lane == 1, i2 - n_groups, 0))
    wt_ref[...] = jnp.where(lane == 0, w1, jnp.where(lane == 1, w2, 0.0))


def moe_router(addends, *, rows, scale, scale_lanes, norm_w, w_route, n_groups, per_group, tm=256):
    n_add = len(addends)
    d = addends[0][0].shape[1]
    in_specs, args = [], []
    for arr, row0 in addends:
        in_specs.append(pl.BlockSpec((tm, d), functools.partial(lambda i, b: (i + b, 0), b=row0 // tm)))
        args.append(arr)
    if scale is not None:
        in_specs.append(pl.BlockSpec((tm, LANES), lambda i: (i, 0)))
        args.append(scale)
    in_specs.append(pl.BlockSpec((1, d), lambda i: (0, 0)))
    args.append(norm_w.reshape(1, d).astype(F32))
    in_specs.append(pl.BlockSpec(w_route.shape, lambda i: (0, 0)))
    args.append(w_route)
    row_spec = pl.BlockSpec((tm, d), lambda i: (i, 0))
    lane_spec = pl.BlockSpec((tm, LANES), lambda i: (i, 0))
    body = functools.partial(_router_body, n_add=n_add, scale_lanes=tuple(scale_lanes),
                             n_groups=n_groups, per_group=per_group)
    return pl.pallas_call(
        body,
        grid=(rows // tm,),
        in_specs=in_specs,
        out_specs=[row_spec, row_spec, lane_spec, lane_spec],
        out_shape=[
            jax.ShapeDtypeStruct((rows, d), F32),
            jax.ShapeDtypeStruct((rows, d), F32),
            jax.ShapeDtypeStruct((rows, LANES), I32),
            jax.ShapeDtypeStruct((rows, LANES), F32),
        ],
        compiler_params=pltpu.CompilerParams(
            dimension_semantics=("arbitrary",), vmem_limit_bytes=VMEM_LIMIT),
        name="moe_router",
    )(*args)


def _experts_body(te_ref, nv_ref, cnt_ref, src_ref, dst_ref, hn_ref, w1_ref, w3_ref, w2_ref, y_ref,
                  xbuf, ybuf, w1b, w3b, w2b, gsem, ssem):
    i = pl.program_id(0)
    nt = pl.num_programs(0)
    tm = xbuf.shape[1]
    n_valid = nv_ref[0]
    slot = i % 2

    def start_gather(tile, s):
        def issue(r, carry):
            tok = src_ref[tile * tm + r]
            pltpu.make_async_copy(hn_ref.at[pl.ds(tok, 1)], xbuf.at[s, pl.ds(r, 1)], gsem.at[s]).start()
            return carry
        lax.fori_loop(0, tm, issue, 0, unroll=8)

    def wait_rows(sem, s, buf, n):
        def wait_full():
            pltpu.make_async_copy(hn_ref.at[pl.ds(0, tm)], buf.at[s], sem.at[s]).wait()

        if isinstance(n, int):
            assert n == tm
            wait_full()
            return
        pl.when(n == tm)(wait_full)

        @pl.when(n < tm)
        def _():
            def one(r, carry):
                pltpu.make_async_copy(hn_ref.at[pl.ds(0, 1)], buf.at[s, pl.ds(0, 1)], sem.at[s]).wait()
                return carry
            lax.fori_loop(0, n, one, 0)

    @pl.when(i == 0)
    def _():
        start_gather(0, 0)

    @pl.when(i + 1 < n_valid)
    def _():
        start_gather(i + 1, 1 - slot)

    @pl.when(i < n_valid)
    def _():
        @pl.when((i == 0) | (te_ref[i] != te_ref[jnp.maximum(i - 1, 0)]))
        def _():
            w1b[...] = w1_ref[0].astype(BF16)
            w3b[...] = w3_ref[0].astype(BF16)
            w2b[...] = w2_ref[0].astype(BF16)

        wait_rows(gsem, slot, xbuf, tm)
        x = xbuf[slot].astype(BF16)
        a = _dot(x, w1b[...])
        u = _dot(x, w3b[...])
        hid = (_silu(a) * u).astype(BF16)
        y = _dot(hid, w2b[...])

        @pl.when(i >= 2)
        def _():
            wait_rows(ssem, slot, ybuf, cnt_ref[jnp.maximum(i - 2, 0)])

        ybuf[slot] = y

        def issue(r, carry):
            row = dst_ref[i * tm + r]
            pltpu.make_async_copy(ybuf.at[slot, pl.ds(r, 1)], y_ref.at[pl.ds(row, 1)], ssem.at[slot]).start()
            return carry

        @pl.when(cnt_ref[i] == tm)
        def _():
            lax.fori_loop(0, tm, issue, 0, unroll=8)

        @pl.when(cnt_ref[i] < tm)
        def _():
            lax.fori_loop(0, cnt_ref[i], issue, 0)

    @pl.when(i == nt - 1)
    def _():
        wait_rows(ssem, (n_valid - 1) % 2, ybuf, cnt_ref[n_valid - 1])

        @pl.when(n_valid >= 2)
        def _():
            wait_rows(ssem, n_valid % 2, ybuf, cnt_ref[jnp.maximum(n_valid - 2, 0)])


def moe_experts(hn, tile_expert, n_valid, tile_rows, src_tok, dst_row, w1, w3, w2, *, out_rows):
    n_tiles = tile_expert.shape[0]
    d = hn.shape[1]
    tm = MOE_TM
    wmap = lambda i, te, nv, cnt, s, dd: (te[i], 0, 0)
    return pl.pallas_call(
        _experts_body,
        grid_spec=pltpu.PrefetchScalarGridSpec(
            num_scalar_prefetch=5,
            grid=(n_tiles,),
            in_specs=[
                pl.BlockSpec(memory_space=pl.ANY),
                pl.BlockSpec((1,) + w1.shape[1:], wmap),
                pl.BlockSpec((1,) + w3.shape[1:], wmap),
                pl.BlockSpec((1,) + w2.shape[1:], wmap),
            ],
            out_specs=pl.BlockSpec(memory_space=pl.ANY),
            scratch_shapes=[
                pltpu.VMEM((2, tm, d), F32),
                pltpu.VMEM((2, tm, d), F32),
                pltpu.VMEM(w1.shape[1:], BF16),
                pltpu.VMEM(w3.shape[1:], BF16),
                pltpu.VMEM(w2.shape[1:], BF16),
                pltpu.SemaphoreType.DMA((2,)),
                pltpu.SemaphoreType.DMA((2,)),
            ],
        ),
        out_shape=jax.ShapeDtypeStruct((out_rows, d), F32),
        compiler_params=pltpu.CompilerParams(
            dimension_semantics=("arbitrary",), vmem_limit_bytes=VMEM_LIMIT),
        name="moe_experts",
    )(tile_expert, n_valid, tile_rows, src_tok, dst_row, hn, w1, w3, w2)


def _dispatch_plan(eid, n_experts, n_tiles):
    t = eid.shape[0]
    tm = MOE_TM
    e_flat = eid.reshape(-1)
    onehot = (e_flat[:, None] == jnp.arange(n_experts, dtype=I32)[None, :]).astype(I32)
    rank = jnp.sum((jnp.cumsum(onehot, axis=0) - onehot) * onehot, axis=1)
    counts = jnp.sum(onehot, axis=0)
    padded = (counts + tm - 1) // tm * tm
    ends = jnp.cumsum(padded)
    offs = ends - padded
    pos = offs[e_flat] + rank
    n_sorted = n_tiles * tm
    src_a = jnp.zeros((n_sorted,), I32).at[pos].set(jnp.arange(t * TOP_K, dtype=I32))
    src_tok = src_a // TOP_K
    dst_row = (src_a % TOP_K) * t + src_tok
    n_valid = (ends[-1] // tm).astype(I32)
    tile_start = jnp.arange(n_tiles, dtype=I32) * tm
    tile_expert = jnp.sum((tile_start[:, None] >= ends[None, :]).astype(I32), axis=1)
    last_e = jnp.sum((jnp.maximum(ends[-1] - tm, 0) >= ends).astype(I32))
    tile_expert = jnp.minimum(tile_expert, last_e).astype(I32)
    tile_rows = jnp.clip(counts[tile_expert] - (tile_start - offs[tile_expert]), 0, tm)
    tile_rows = jnp.where(tile_start < ends[-1], tile_rows, 0).astype(I32)
    return tile_expert, n_valid.reshape(1), tile_rows, src_tok, dst_row


def moe_layer(addends, scale, scale_lanes, norm_w, w_group, w_expert, w1, w3, w2, *, rows):
    d = w_group.shape[0]
    n_groups = w_group.shape[1]
    n_experts = w_expert.shape[1]
    w_route = jnp.zeros((d, LANES), F32).at[:, :n_groups].set(w_group).at[:, n_groups:n_groups + n_experts].set(w_expert)
    x, hn, idx, wts = moe_router(addends, rows=rows, scale=scale, scale_lanes=scale_lanes, norm_w=norm_w,
                                 w_route=w_route, n_groups=n_groups, per_group=n_experts // n_groups)
    n_tiles = (rows * TOP_K + n_experts * (MOE_TM - 1) + MOE_TM - 1) // MOE_TM
    plan = _dispatch_plan(idx[:, :TOP_K], n_experts, n_tiles)
    y2 = moe_experts(hn, *plan, w1, w3, w2, out_rows=TOP_K * rows)
    return x, wts, y2


def kernel(x_prompt, x_sample, state_ssm, state_conv, cache_k, cache_v, page_table, meta_tokens,
           norm_mix, norm_ffn, norm_final, m_w_in, m_conv_w, m_conv_b, m_dt_bias, m_A_log, m_D,
           m_norm_w, m_w_out, a_w_qkv, a_w_o, a_logit_bias, moe_w_group, moe_w_expert, moe_w1, moe_w3, moe_w2):
    bp, seq, d = x_prompt.shape
    nb, dec, _ = x_sample.shape
    n_meta = meta_tokens.shape[0]
    assert bp == 1 and dec == SUBLANES
    d_inner = m_w_out.shape[0]
    n_heads_ssm = m_A_log.shape[0]
    conv_dim = m_conv_w.shape[1]
    d_state = state_ssm.shape[-1]
    hd_ssm = d_inner // n_heads_ssm
    n_heads = a_logit_bias.shape[0]
    hd = d // n_heads

    p_len = n_meta + seq
    n_pad = (-p_len) % SSD_CHUNK
    tp = n_pad + p_len
    tq = _round_up(tp, ATTN_TQ)
    ns = nb * dec
    row_s = tq
    rows = _round_up(row_s + ns, ROW_TILE)

    x0 = jnp.concatenate([
        jnp.zeros((n_pad, d), F32), meta_tokens.astype(F32), x_prompt[0],
        jnp.zeros((row_s - tp, d), F32), x_sample.reshape(ns, d),
        jnp.zeros((rows - row_s - ns, d), F32)], axis=0)

    w_in = m_w_in.astype(BF16)
    w_z = w_in[:, :d_inner]
    w_xbc = w_in[:, d_inner:d_inner + conv_dim]
    w_dt = jnp.zeros((d, LANES), BF16).at[:, :n_heads_ssm].set(w_in[:, d_inner + conv_dim:])
    z, xbc, dtr = fused_linear([(x0, 0)], rows=rows, norm_w=norm_mix[0], weights=(w_z, w_xbc, w_dt),
                               out_dtypes=((F32,), (F32,), (F32,)), name="in_proj")

    head_of_lane = jnp.arange(d_inner, dtype=I32) // hd_ssm
    e_sel = (jnp.arange(LANES, dtype=I32)[:, None] == head_of_lane[None, :]).astype(BF16)
    pad_h = LANES - n_heads_ssm
    ssd_params = dict(
        n_heads=n_heads_ssm,
        conv_w=m_conv_w.astype(F32), conv_b=m_conv_b.reshape(1, conv_dim).astype(F32),
        dt_bias=jnp.pad(m_dt_bias.astype(F32), (0, pad_h)).reshape(1, LANES),
        a_log=jnp.pad(m_A_log.astype(F32), (0, pad_h)).reshape(1, LANES),
        d_skip=jnp.repeat(m_D.astype(F32), hd_ssm).reshape(1, d_inner),
        norm_w=m_norm_w.reshape(1, d_inner).astype(F32),
        e_sel=e_sel, e_sel_t=e_sel.T)
    g_p, ssm_p = ssd_scan(xbc[None], dtr[None], z[None], jnp.zeros((1, d_inner, d_state), F32),
                          ssd_params, n_rows=tp, n_first=n_pad, name="ssd_prompt")

    L = SSD_CHUNK
    kw = state_conv.shape[1]
    lead = L - dec - kw

    def seq_chunk(a, pre):
        a = a[row_s:row_s + ns].reshape(nb, dec, a.shape[1])
        return jnp.concatenate([jnp.zeros((nb, lead, a.shape[2]), F32), pre, a], axis=1)

    xbc_s = seq_chunk(xbc, state_conv.astype(F32))
    dt_s = seq_chunk(dtr, jnp.zeros((nb, kw, LANES), F32))
    z_s = seq_chunk(z, jnp.zeros((nb, kw, d_inner), F32))
    g_s, ssm_s = ssd_scan(xbc_s, dt_s, z_s, state_ssm.reshape(nb, d_inner, d_state).astype(F32),
                          ssd_params, n_rows=L, n_first=L - dec, name="ssd_sample")

    g = jnp.concatenate([g_p[0], jnp.zeros((row_s - tp, d_inner), BF16), g_s[:, L - dec:].reshape(ns, d_inner),
                         jnp.zeros((rows - row_s - ns, d_inner), BF16)], axis=0)
    (x1,) = fused_linear([(g, 0)], rows=rows, weights=(m_w_out.astype(BF16),), out_dtypes=((F32,),),
                         residual=x0, name="out_proj")

    conv_p = xbc[tp - kw:tp][None]
    conv_s = xbc[row_s:row_s + ns].reshape(nb, dec, conv_dim)[:, dec - kw:]

    x1, wts0, y0 = moe_layer([(x1, 0)], None, (None,), norm_ffn[0], moe_w_group[0], moe_w_expert[0],
                             moe_w1[0], moe_w3[0], moe_w2[0], rows=rows)

    qscale = (hd ** -0.5) * LOG2E
    w_q = (a_w_qkv[:, :d] * qscale).astype(BF16)
    w_k = a_w_qkv[:, d:2 * d].astype(BF16)
    w_v = a_w_qkv[:, 2 * d:].astype(BF16)
    x2, qf, qb, kf, kb, vf, vb = fused_linear(
        [(x1, 0), (y0, 0), (y0, rows)], rows=rows, scale=wts0, scale_lanes=(None, 0, 1), norm_w=norm_mix[1],
        weights=(w_q, w_k, w_v), out_dtypes=((F32, BF16), (F32, BF16), (F32, BF16)), emit_x=True, name="qkv_proj")
    bias2 = a_logit_bias.astype(F32) * LOG2E
    aux_w = LANES - hd
    b_hi = bias2.astype(BF16)
    b_lo = (bias2 - b_hi.astype(F32)).astype(BF16)
    aux_q = jnp.zeros((n_heads, aux_w), BF16).at[:, 0].set(b_hi).at[:, 1].set(b_lo).at[:, 2].set(1.0)
    q_aug = jnp.concatenate([qb.reshape(rows, n_heads, hd),
                             jnp.broadcast_to(aux_q[None], (rows, n_heads, aux_w))], axis=2)
    pad_mask = jnp.where(jnp.arange(rows) < n_pad, MASKED_LOGIT, 0.0).astype(BF16)
    aux_k = jnp.concatenate([jnp.ones((rows, n_heads, 2), BF16),
                             jnp.broadcast_to(pad_mask[:, None, None], (rows, n_heads, 1)),
                             jnp.zeros((rows, n_heads, aux_w - 3), BF16)], axis=2)
    k_aug = jnp.concatenate([kb.reshape(rows, n_heads, hd), aux_k], axis=2)
    o_p = attn_prompt(q_aug.reshape(rows, n_heads * LANES), k_aug.reshape(rows, n_heads * LANES), vb,
                      rows=tq, hd=hd)
    bias_col = jnp.repeat(bias2, dec).reshape(n_heads * dec, 1)
    o_s = attn_sample(qf, kf, vf, row_s, cache_k, cache_v, page_table, bias_col, n_heads=n_heads, hd=hd)
    o = jnp.concatenate([o_p, o_s, jnp.zeros((rows - row_s - ns, d), BF16)], axis=0)
    (x3,) = fused_linear([(o, 0)], rows=rows, weights=(a_w_o.astype(BF16),), out_dtypes=((F32,),),
                         residual=x2, name="attn_out_proj")

    x3, wts1, y1 = moe_layer([(x3, 0)], None, (None,), norm_ffn[1], moe_w_group[1], moe_w_expert[1],
                             moe_w1[1], moe_w3[1], moe_w2[1], rows=rows)

    (yn,) = fused_linear([(x3, 0), (y1, 0), (y1, rows)], rows=rows, scale=wts1, scale_lanes=(None, 0, 1),
                         norm_w=norm_final, emit_norm=True, name="final_norm")

    y_prompt = yn[n_pad + n_meta:tp][None]
    y_sample = yn[row_s:row_s + ns].reshape(nb, dec, d)
    k_p = kf[n_pad:tp].reshape(1, p_len, n_heads, hd)
    v_p = vf[n_pad:tp].reshape(1, p_len, n_heads, hd)
    k_s = kf[row_s:row_s + ns].reshape(nb, dec, n_heads, hd)
    v_s = vf[row_s:row_s + ns].reshape(nb, dec, n_heads, hd)
    return (y_prompt, y_sample, ssm_p.reshape(1, n_heads_ssm, hd_ssm, d_state), conv_p,
            k_p, v_p, ssm_s.reshape(nb, n_heads_ssm, hd_ssm, d_state), conv_s, k_s, v_s)
```

```python
import functools
import math

import jax
import jax.numpy as jnp
from jax import lax
from jax.experimental import pallas as pl
from jax.experimental.pallas import tpu as pltpu

F32 = jnp.float32
BF16 = jnp.bfloat16
I32 = jnp.int32

RMS_EPS = 1e-6
LOG2E = 1.4426950408889634
NEG_BIG = -3.0e38

LANES = 128
SUBLANES = 8
VMEM_LIMIT = 48 * 1024 * 1024

SSD_CHUNK = 128
ROW_TILE = 512
ATTN_TQ = 256
ATTN_TK = 128
ATTN_HEADS_PER_STEP = 4
MOE_TM = 256
TOP_K = 2
PAGES_PER_STEP = 8


def _round_up(x, m):
    return (x + m - 1) // m * m


def _dot(a, b):
    return jnp.dot(a, b, preferred_element_type=F32)


def _dot_nt(a, b):
    return lax.dot_general(a, b, (((1,), (1,)), ((), ())), preferred_element_type=F32)


def _split3(x):
    hi = x.astype(BF16)
    r1 = x - hi.astype(F32)
    mid = r1.astype(BF16)
    lo = (r1 - mid.astype(F32)).astype(BF16)
    return hi, mid, lo


def _dot_exact_rhs(sel, x):
    hi, mid, lo = _split3(x)
    return _dot(sel, hi) + _dot(sel, mid) + _dot(sel, lo)


def _dot_exact_lhs(x, sel):
    hi, mid, lo = _split3(x)
    return _dot(hi, sel) + _dot(mid, sel) + _dot(lo, sel)


def _silu(x):
    return x / (1.0 + jnp.exp(-x))


def _softplus(x):
    return jnp.maximum(x, 0.0) + jnp.log1p(jnp.exp(-jnp.abs(x)))


def _linear_body(*refs, n_add, scale_lanes, has_scale, has_norm, n_w, has_res,
                 emit_x, emit_norm, n_outs, tn):
    it = iter(refs)
    add_refs = [next(it) for _ in range(n_add)]
    sc_ref = next(it) if has_scale else None
    nw_ref = next(it) if has_norm else None
    w_refs = [next(it) for _ in range(n_w)]
    res_ref = next(it) if has_res else None
    xo_ref = next(it) if emit_x else None
    no_ref = next(it) if emit_norm else None
    out_refs = [[next(it) for _ in range(k)] for k in n_outs]
    xb_ref = next(it) if n_w else None

    x = None
    for r, lane in zip(add_refs, scale_lanes):
        v = r[...]
        if lane is not None:
            v = v * sc_ref[:, lane:lane + 1]
        x = v if x is None else x + v
    if emit_x:
        xo_ref[...] = x
    if has_norm:
        ms = jnp.mean(x * x, axis=-1, keepdims=True)
        x = (x * lax.rsqrt(ms + RMS_EPS)) * nw_ref[...]
        if emit_norm:
            no_ref[...] = x
    if not n_w:
        return
    xb_ref[...] = x.astype(BF16)
    for k, w_ref in enumerate(w_refs):
        n_cols = w_ref.shape[1]
        for n0 in range(0, n_cols, tn):
            n1 = min(n0 + tn, n_cols)
            r = _dot(xb_ref[...], w_ref[:, n0:n1])
            if has_res and k == 0:
                r = r + res_ref[:, n0:n1]
            for o_ref in out_refs[k]:
                o_ref[:, n0:n1] = r.astype(o_ref.dtype)


def fused_linear(addends, *, rows, scale=None, scale_lanes=None, norm_w=None, weights=(),
                 out_dtypes=(), residual=None, emit_x=False, emit_norm=False,
                 tm=256, tn=512, name="fused_linear"):
    n_add = len(addends)
    if scale_lanes is None:
        scale_lanes = (None,) * n_add
    d_in = addends[0][0].shape[1]
    assert rows % tm == 0
    in_specs, args = [], []
    for arr, row0 in addends:
        assert row0 % tm == 0
        in_specs.append(pl.BlockSpec((tm, d_in), functools.partial(lambda i, b: (i + b, 0), b=row0 // tm)))
        args.append(arr)
    if scale is not None:
        in_specs.append(pl.BlockSpec((tm, LANES), lambda i: (i, 0)))
        args.append(scale)
    if norm_w is not None:
        in_specs.append(pl.BlockSpec((1, d_in), lambda i: (0, 0)))
        args.append(norm_w.reshape(1, d_in).astype(F32))
    for w in weights:
        in_specs.append(pl.BlockSpec(w.shape, lambda i: (0, 0)))
        args.append(w)
    if residual is not None:
        in_specs.append(pl.BlockSpec((tm, residual.shape[1]), lambda i: (i, 0)))
        args.append(residual)
    out_shapes, out_specs = [], []
    if emit_x:
        out_shapes.append(jax.ShapeDtypeStruct((rows, d_in), F32))
        out_specs.append(pl.BlockSpec((tm, d_in), lambda i: (i, 0)))
    if emit_norm:
        out_shapes.append(jax.ShapeDtypeStruct((rows, d_in), F32))
        out_specs.append(pl.BlockSpec((tm, d_in), lambda i: (i, 0)))
    for w, dts in zip(weights, out_dtypes):
        for dt in dts:
            out_shapes.append(jax.ShapeDtypeStruct((rows, w.shape[1]), dt))
            out_specs.append(pl.BlockSpec((tm, w.shape[1]), lambda i: (i, 0)))
    scratch = [pltpu.VMEM((tm, d_in), BF16)] if weights else []
    body = functools.partial(
        _linear_body, n_add=n_add, scale_lanes=tuple(scale_lanes), has_scale=scale is not None,
        has_norm=norm_w is not None, n_w=len(weights), has_res=residual is not None,
        emit_x=emit_x, emit_norm=emit_norm, n_outs=tuple(len(d) for d in out_dtypes), tn=tn)
    return pl.pallas_call(
        body,
        grid=(rows // tm,),
        in_specs=in_specs,
        out_specs=out_specs,
        out_shape=out_shapes,
        scratch_shapes=scratch,
        compiler_params=pltpu.CompilerParams(
            dimension_semantics=("arbitrary",), vmem_limit_bytes=VMEM_LIMIT),
        name=name,
    )(*args)


def _ssd_body(xbc_ref, dt_ref, z_ref, s0_ref, cw_ref, cb_ref, dtb_ref, alog_ref, dsk_ref,
              nw_ref, e_ref, et_ref, g_ref, sout_ref, ext_ref, s_ref, y_ref, *,
              n_first, d_inner, n_groups, d_state, n_heads):
    c = pl.program_id(1)
    L = SSD_CHUNK
    hd = d_inner // n_heads
    gw = n_groups * d_state
    hpg = n_heads // n_groups

    @pl.when(c == 0)
    def _():
        ext_ref[0:SUBLANES, :] = jnp.zeros((SUBLANES, ext_ref.shape[1]), F32)
        s_ref[...] = s0_ref[0]

    ext_ref[SUBLANES:SUBLANES + L, :] = xbc_ref[0]
    conv = cb_ref[...] + ext_ref[5:5 + L, :] * cw_ref[0:1, :]
    conv = conv + ext_ref[6:6 + L, :] * cw_ref[1:2, :]
    conv = conv + ext_ref[7:7 + L, :] * cw_ref[2:3, :]
    conv = conv + ext_ref[8:8 + L, :] * cw_ref[3:4, :]
    ext_ref[0:SUBLANES, :] = ext_ref[L:L + SUBLANES, :]

    row = lax.broadcasted_iota(I32, (L, 1), 0)
    n0 = jnp.where(c == 0, n_first, 0)
    valid = (row >= n0).astype(F32)
    xbc = _silu(conv) * valid
    xs = xbc[:, :d_inner]

    dt = _softplus(dt_ref[0] + dtb_ref[...]) * valid
    a = dt * (-jnp.exp(alog_ref[...]))
    ri = lax.broadcasted_iota(I32, (L, L), 0)
    ci = lax.broadcasted_iota(I32, (L, L), 1)
    tri = ri >= ci
    acs = _dot_exact_rhs(tri.astype(BF16), a)
    acs_t = acs.T
    acs_last = acs[L - 1:L, :]

    e_sel = e_ref[...]
    dt_e = _dot_exact_lhs(dt, e_sel)
    grow_e = _dot_exact_lhs(jnp.exp(acs), e_sel)
    dec_e = _dot_exact_lhs(jnp.exp(acs_last - acs), e_sel)
    xdt = xs * dt_e
    xw = xdt * dec_e

    lane = lax.broadcasted_iota(I32, (L, LANES), 1)
    lo_half = lane < hd
    s_new = []
    for g in range(n_groups):
        bg = xbc[:, d_inner + g * d_state:d_inner + (g + 1) * d_state].astype(BF16)
        cg = xbc[:, d_inner + gw + g * d_state:d_inner + gw + (g + 1) * d_state].astype(BF16)
        cb = _dot_nt(cg, bg)
        for j in range(hpg // 2):
            h0 = g * hpg + 2 * j
            ms = []
            for h in (h0, h0 + 1):
                diff = acs[:, h:h + 1] - acs_t[h:h + 1, :]
                ms.append((cb * jnp.where(tri, jnp.exp(diff), 0.0)).astype(BF16))
            lhs = jnp.concatenate(ms, axis=1)
            xp = xdt[:, h0 * hd:(h0 + 2) * hd]
            rhs = jnp.concatenate([jnp.where(lo_half, xp, 0.0), jnp.where(lo_half, 0.0, xp)],
                                  axis=0).astype(BF16)
            y_ref[:, h0 * hd:(h0 + 2) * hd] = _dot(lhs, rhs)
        c0, c1 = g * hpg * hd, (g + 1) * hpg * hd
        s_g = s_ref[c0:c1, :]
        y_off = _dot_nt(cg, s_g.astype(BF16)) * grow_e[:, c0:c1]
        y_ref[:, c0:c1] = y_ref[:, c0:c1] + y_off
        s_new.append(_dot(xw[:, c0:c1].T.astype(BF16), bg))

    last_col = jnp.broadcast_to(acs_t[:, L - 1:L], (LANES, LANES))
    cdec = jnp.exp(_dot_exact_rhs(et_ref[...], last_col))
    s_ref[...] = s_ref[...] * cdec + jnp.concatenate(s_new, axis=0)

    y = y_ref[...] + xs * dsk_ref[...]
    gt = y * _silu(z_ref[0])
    gsz = d_inner // n_groups
    outs = []
    for g in range(n_groups):
        gg = gt[:, g * gsz:(g + 1) * gsz]
        ms = jnp.mean(gg * gg, axis=-1, keepdims=True)
        outs.append(gg * lax.rsqrt(ms + RMS_EPS))
    g_ref[0] = (jnp.concatenate(outs, axis=1) * nw_ref[...]).astype(g_ref.dtype)

    @pl.when(c == pl.num_programs(1) - 1)
    def _():
        sout_ref[0] = s_ref[...]


def ssd_scan(xbc, dt, z, s0, params, *, n_rows, n_first, name):
    nb, _, conv_dim = xbc.shape
    t = n_rows
    d_inner = z.shape[2]
    d_state = s0.shape[2]
    n_heads = params["n_heads"]
    n_groups = (conv_dim - d_inner) // (2 * d_state)
    assert t % SSD_CHUNK == 0 and (d_inner // n_heads) * 2 == LANES
    nc = t // SSD_CHUNK
    L = SSD_CHUNK
    const = lambda b, c: (0, 0)
    body = functools.partial(_ssd_body, n_first=n_first, d_inner=d_inner, n_groups=n_groups,
                             d_state=d_state, n_heads=n_heads)
    return pl.pallas_call(
        body,
        grid=(nb, nc),
        in_specs=[
            pl.BlockSpec((1, L, conv_dim), lambda b, c: (b, c, 0)),
            pl.BlockSpec((1, L, LANES), lambda b, c: (b, c, 0)),
            pl.BlockSpec((1, L, d_inner), lambda b, c: (b, c, 0)),
            pl.BlockSpec((1, d_inner, d_state), lambda b, c: (b, 0, 0)),
            pl.BlockSpec(params["conv_w"].shape, const),
            pl.BlockSpec(params["conv_b"].shape, const),
            pl.BlockSpec(params["dt_bias"].shape, const),
            pl.BlockSpec(params["a_log"].shape, const),
            pl.BlockSpec(params["d_skip"].shape, const),
            pl.BlockSpec(params["norm_w"].shape, const),
            pl.BlockSpec(params["e_sel"].shape, const),
            pl.BlockSpec(params["e_sel_t"].shape, const),
        ],
        out_specs=[
            pl.BlockSpec((1, L, d_inner), lambda b, c: (b, c, 0)),
            pl.BlockSpec((1, d_inner, d_state), lambda b, c: (b, 0, 0)),
        ],
        out_shape=[
            jax.ShapeDtypeStruct((nb, t, d_inner), BF16),
            jax.ShapeDtypeStruct((nb, d_inner, d_state), F32),
        ],
        scratch_shapes=[
            pltpu.VMEM((L + SUBLANES, conv_dim), F32),
            pltpu.VMEM((d_inner, d_state), F32),
            pltpu.VMEM((L, d_inner), F32),
        ],
        compiler_params=pltpu.CompilerParams(
            dimension_semantics=("arbitrary", "arbitrary"), vmem_limit_bytes=VMEM_LIMIT),
        name=name,
    )(xbc, dt, z, s0, params["conv_w"], params["conv_b"], params["dt_bias"], params["a_log"],
      params["d_skip"], params["norm_w"], params["e_sel"], params["e_sel_t"])


MASKED_LOGIT = -1e30


def _suffix_sum_matrix(tk):
    kr = lax.broadcasted_iota(I32, (tk, tk), 0)
    kc = lax.broadcasted_iota(I32, (tk, tk), 1)
    return jnp.where(kr >= kc, -1.0, 0.0).astype(BF16)


def _sb_stage_a(z, vis):
    neg_abs = pltpu.bitcast(pltpu.bitcast(z, jnp.uint32) | jnp.uint32(0x80000000), F32)
    sp = jnp.maximum(z, 0.0) + jnp.log2(1.0 + jnp.exp2(neg_abs))
    if vis is not None:
        sp = jnp.where(vis, sp, 0.0)
        z = jnp.where(vis, z, MASKED_LOGIT)
    return z, sp.astype(BF16)


def _sb_stage_b(z, sp_split, carry, u2neg):
    suffix = _dot(sp_split, u2neg)
    return jnp.exp2(z + suffix + carry), carry + suffix[:, 0:1]


def _attn_prompt_body(q_ref, k_ref, v_ref, o_ref, z_scr, sp_scr, *, hd):
    qi = pl.program_id(1)
    tq, tk = ATTN_TQ, ATTN_TK
    nh = ATTN_HEADS_PER_STEP
    u2neg = _suffix_sum_matrix(tk)
    lo_v = lax.broadcasted_iota(I32, (tk, LANES), 1) < hd
    qrow = qi * tq + lax.broadcasted_iota(I32, (tq, tk), 0)
    kidx = lax.broadcasted_iota(I32, (tq, tk), 1)

    def stage_a(j, kb, masked):
        k0 = pl.multiple_of(kb * tk, tk)
        vis = (k0 + kidx) < qrow if masked else None
        for hh in range(nh):
            cols = slice(hh * LANES, (hh + 1) * LANES)
            z, sp = _sb_stage_a(_dot_nt(q_ref[:, cols], k_ref[pl.ds(k0, tk), cols]), vis)
            z_scr[j, hh] = z
            sp_scr[j, hh] = sp

    def stage_b(j, kb, carries, accs):
        k0 = pl.multiple_of(kb * tk, tk)
        carries, accs = list(carries), list(accs)
        for p in range(nh // 2):
            vblk = v_ref[pl.ds(k0, tk), p * LANES:(p + 1) * LANES]
            ws = []
            for hh in (2 * p, 2 * p + 1):
                w, carries[hh] = _sb_stage_b(z_scr[j, hh], sp_scr[j, hh], carries[hh], u2neg)
                ws.append(w.astype(BF16))
            vcat = jnp.concatenate([jnp.where(lo_v, vblk, 0), jnp.where(lo_v, 0, vblk)], axis=0)
            accs[p] = accs[p] + _dot(jnp.concatenate(ws, axis=1), vcat)
        return tuple(carries), tuple(accs)

    nd = tq // tk
    n_full = qi * nd
    carries = tuple(jnp.zeros((tq, 1), F32) for _ in range(nh))
    accs = tuple(jnp.zeros((tq, LANES), F32) for _ in range(nh // 2))
    for j in range(nd):
        stage_a(j, n_full + nd - 1 - j, True)

    def step(t, state):
        carries, accs = state
        kb0 = n_full - 1 - nd * t
        for j in range(nd):
            carries, accs = stage_b(j, kb0 + nd - j, carries, accs)
        for j in range(nd):
            stage_a(j, kb0 - j, False)
        return carries, accs

    carries, accs = lax.fori_loop(0, qi, step, (carries, accs))
    for j in range(nd):
        carries, accs = stage_b(j, nd - 1 - j, carries, accs)
    for p in range(nh // 2):
        o_ref[:, p * LANES:(p + 1) * LANES] = accs[p].astype(o_ref.dtype)


def attn_prompt(q_aug, k_aug, vb, *, rows, hd):
    d = vb.shape[1]
    nh = ATTN_HEADS_PER_STEP
    n_heads = d // hd
    assert ATTN_TQ % ATTN_TK == 0 and n_heads % nh == 0 and nh % 2 == 0 and 2 * hd == LANES
    body = functools.partial(_attn_prompt_body, hd=hd)
    resident = dict(pipeline_mode=pl.Buffered(1))
    return pl.pallas_call(
        body,
        grid=(n_heads // nh, rows // ATTN_TQ),
        in_specs=[
            pl.BlockSpec((ATTN_TQ, nh * LANES), lambda g, i: (i, g)),
            pl.BlockSpec((rows, nh * LANES), lambda g, i: (0, g), **resident),
            pl.BlockSpec((rows, nh * hd), lambda g, i: (0, g), **resident),
        ],
        out_specs=pl.BlockSpec((ATTN_TQ, nh * hd), lambda g, i: (i, g)),
        out_shape=jax.ShapeDtypeStruct((rows, d), BF16),
        scratch_shapes=[
            pltpu.VMEM((ATTN_TQ // ATTN_TK, nh, ATTN_TQ, ATTN_TK), F32),
            pltpu.VMEM((ATTN_TQ // ATTN_TK, nh, ATTN_TQ, ATTN_TK), BF16),
        ],
        compiler_params=pltpu.CompilerParams(
            dimension_semantics=("arbitrary", "arbitrary"), vmem_limit_bytes=VMEM_LIMIT),
        name="attn_prompt",
    )(q_aug, k_aug, vb)


def _attn_sample_body(pt_ref, q_ref, kn_ref, vn_ref, bias_ref, *rest, n_heads, hd, pps, dec):
    k_refs = rest[:pps]
    v_refs = rest[pps:2 * pps]
    o_ref = rest[2 * pps]
    qbd_ref, kpad_ref, vpad_ref, acc_ref, carry_ref = rest[2 * pps + 1:]
    j = pl.program_id(1)
    nr = n_heads * dec
    ps = k_refs[0].shape[2]
    u2neg = _suffix_sum_matrix(ps)
    b2 = bias_ref[...]

    def process(kblk, vblk, vis):
        z = _dot(qbd_ref[...], kblk) + b2
        w, carry = _sb_stage_b(*_sb_stage_a(z, vis), carry_ref[...], u2neg)
        carry_ref[...] = carry
        acc_ref[...] += _dot_nt(w.astype(BF16), vblk)

    @pl.when(j == 0)
    def _():
        d = q_ref.shape[1]
        qt = jnp.concatenate([q_ref[...]] * n_heads, axis=0)
        rh = lax.broadcasted_iota(I32, (nr, d), 0) // dec
        lh = lax.broadcasted_iota(I32, (nr, d), 1) // hd
        qbd_ref[...] = jnp.where(rh == lh, qt, 0.0).astype(BF16)
        kpad_ref[...] = jnp.zeros(kpad_ref.shape, F32)
        vpad_ref[...] = jnp.zeros(vpad_ref.shape, F32)
        kpad_ref[0:dec, :] = kn_ref[...]
        vpad_ref[0:dec, :] = vn_ref[...]
        acc_ref[...] = jnp.zeros(acc_ref.shape, F32)
        carry_ref[...] = jnp.zeros(carry_ref.shape, F32)
        qrow = lax.broadcasted_iota(I32, (nr, ps), 0) % dec
        kidx = lax.broadcasted_iota(I32, (nr, ps), 1)
        process(kpad_ref[...].T.astype(BF16), vpad_ref[...].T.astype(BF16), kidx < qrow)

    @pl.when(j > 0)
    def _():
        order = range(pps - 1, -1, -1)
        zs = [_dot(qbd_ref[...], k_refs[i][0].astype(BF16)) + b2 for i in order]
        carry = carry_ref[...]
        ws = []
        for z in zs:
            w, carry = _sb_stage_b(*_sb_stage_a(z, None), carry, u2neg)
            ws.append(w.astype(BF16))
        carry_ref[...] = carry
        vcat = jnp.concatenate([v_refs[i][0].astype(BF16) for i in order], axis=1)
        acc_ref[...] += _dot_nt(jnp.concatenate(ws, axis=1), vcat)

    @pl.when(j == pl.num_programs(1) - 1)
    def _():
        d = o_ref.shape[1]
        lh = lax.broadcasted_iota(I32, (dec, d), 1) // hd
        out = jnp.zeros((dec, d), F32)
        for h in range(n_heads):
            out = out + jnp.where(lh == h, acc_ref[h * dec:(h + 1) * dec, :], 0.0)
        o_ref[...] = out.astype(o_ref.dtype)


def attn_sample(q, k, v, row0, cache_k, cache_v, page_table, bias_col, *, n_heads, hd):
    nb, n_pages = page_table.shape
    n_pool, ps = cache_k.shape[0], cache_k.shape[1]
    d = n_heads * hd
    dec = SUBLANES
    nr = n_heads * dec
    assert nr == ps == LANES and row0 % dec == 0
    pps = min(PAGES_PER_STEP, n_pages)
    assert n_pages % pps == 0
    n_steps = n_pages // pps
    ck = jnp.transpose(cache_k, (0, 2, 3, 1)).reshape(n_pool, d, ps)
    cv = jnp.transpose(cache_v, (0, 2, 3, 1)).reshape(n_pool, d, ps)
    pt = page_table.reshape(-1).astype(I32)
    rb0 = row0 // dec

    def page_map(b, j, pt_ref, *, i):
        jj = jnp.maximum(j, 1)
        return (pt_ref[b * n_pages + n_pages - jj * pps + i], 0, 0)

    new_map = lambda b, j, pt_ref: (rb0 + b, 0)
    in_specs = [
        pl.BlockSpec((dec, d), new_map),
        pl.BlockSpec((dec, d), new_map),
        pl.BlockSpec((dec, d), new_map),
        pl.BlockSpec((nr, 1), lambda b, j, pt_ref: (0, 0)),
    ]
    in_specs += [pl.BlockSpec((1, d, ps), functools.partial(page_map, i=i)) for i in range(pps)]
    in_specs += [pl.BlockSpec((1, d, ps), functools.partial(page_map, i=i)) for i in range(pps)]
    body = functools.partial(_attn_sample_body, n_heads=n_heads, hd=hd, pps=pps, dec=dec)
    return pl.pallas_call(
        body,
        grid_spec=pltpu.PrefetchScalarGridSpec(
            num_scalar_prefetch=1,
            grid=(nb, n_steps + 1),
            in_specs=in_specs,
            out_specs=pl.BlockSpec((dec, d), lambda b, j, pt_ref: (b, 0)),
            scratch_shapes=[
                pltpu.VMEM((nr, d), BF16),
                pltpu.VMEM((ps, d), F32),
                pltpu.VMEM((ps, d), F32),
                pltpu.VMEM((nr, d), F32),
                pltpu.VMEM((nr, 1), F32),
            ],
        ),
        out_shape=jax.ShapeDtypeStruct((nb * dec, d), BF16),
        compiler_params=pltpu.CompilerParams(
            dimension_semantics=("arbitrary", "arbitrary"), vmem_limit_bytes=VMEM_LIMIT),
        name="attn_sample",
    )(pt, q, k, v, bias_col, *([ck] * pps), *([cv] * pps))


def _router_body(*refs, n_add, scale_lanes, n_groups, per_group):
    it = iter(refs)
    add_refs = [next(it) for _ in range(n_add)]
    sc_ref = next(it) if any(l is not None for l in scale_lanes) else None
    nw_ref = next(it)
    wr_ref = next(it)
    x_ref, hn_ref, idx_ref, wt_ref = next(it), next(it), next(it), next(it)

    x = None
    for r, ln in zip(add_refs, scale_lanes):
        v = r[...]
        if ln is not None:
            v = v * sc_ref[:, ln:ln + 1]
        x = v if x is None else x + v
    x_ref[...] = x
    ms = jnp.mean(x * x, axis=-1, keepdims=True)
    hn = (x * lax.rsqrt(ms + RMS_EPS)) * nw_ref[...]
    hn_ref[...] = hn
    logits = jnp.dot(hn, wr_ref[...], preferred_element_type=F32, precision=lax.Precision.HIGHEST)

    tm = logits.shape[0]
    lane = lax.broadcasted_iota(I32, (tm, LANES), 1)
    far = jnp.int32(4 * LANES)
    gl = jnp.where(lane < n_groups, logits, NEG_BIG)
    gmax = jnp.max(gl, axis=1, keepdims=True)
    gidx = jnp.min(jnp.where(gl == gmax, lane, far), axis=1, keepdims=True)
    gsum = jnp.sum(jnp.where(lane < n_groups, jnp.exp(logits - gmax), 0.0), axis=1, keepdims=True)
    g_w = 1.0 / gsum
    lo = n_groups + gidx * per_group
    el = jnp.where(lane >= lo, jnp.where(lane < lo + per_group, logits, NEG_BIG), NEG_BIG)
    v1 = jnp.max(el, axis=1, keepdims=True)
    i1 = jnp.min(jnp.where(el == v1, lane, far), axis=1, keepdims=True)
    el2 = jnp.where(lane == i1, NEG_BIG, el)
    v2 = jnp.max(el2, axis=1, keepdims=True)
    i2 = jnp.min(jnp.where(el2 == v2, lane, far), axis=1, keepdims=True)
    e2 = jnp.exp(v2 - v1)
    den = 1.0 + e2
    w1 = (1.0 / den) * g_w
    w2 = (e2 / den) * g_w
    idx_ref[...] = jnp.where(lane == 0, i1 - n_groups, jnp.where(lane == 1, i2 - n_groups, 0))
    wt_ref[...] = jnp.where(lane == 0, w1, jnp.where(lane == 1, w2, 0.0))


def moe_router(addends, *, rows, scale, scale_lanes, norm_w, w_route, n_groups, per_group, tm=256):
    n_add = len(addends)
    d = addends[0][0].shape[1]
    in_specs, args = [], []
    for arr, row0 in addends:
        in_specs.append(pl.BlockSpec((tm, d), functools.partial(lambda i, b: (i + b, 0), b=row0 // tm)))
        args.append(arr)
    if scale is not None:
        in_specs.append(pl.BlockSpec((tm, LANES), lambda i: (i, 0)))
        args.append(scale)
    in_specs.append(pl.BlockSpec((1, d), lambda i: (0, 0)))
    args.append(norm_w.reshape(1, d).astype(F32))
    in_specs.append(pl.BlockSpec(w_route.shape, lambda i: (0, 0)))
    args.append(w_route)
    row_spec = pl.BlockSpec((tm, d), lambda i: (i, 0))
    lane_spec = pl.BlockSpec((tm, LANES), lambda i: (i, 0))
    body = functools.partial(_router_body, n_add=n_add, scale_lanes=tuple(scale_lanes),
                             n_groups=n_groups, per_group=per_group)
    return pl.pallas_call(
        body,
        grid=(rows // tm,),
        in_specs=in_specs,
        out_specs=[row_spec, row_spec, lane_spec, lane_spec],
        out_shape=[
            jax.ShapeDtypeStruct((rows, d), F32),
            jax.ShapeDtypeStruct((rows, d), F32),
            jax.ShapeDtypeStruct((rows, LANES), I32),
            jax.ShapeDtypeStruct((rows, LANES), F32),
        ],
        compiler_params=pltpu.CompilerParams(
            dimension_semantics=("arbitrary",), vmem_limit_bytes=VMEM_LIMIT),
        name="moe_router",
    )(*args)


def _experts_body(te_ref, nv_ref, cnt_ref, src_ref, dst_ref, hn_ref, w1_ref, w3_ref, w2_ref, y_ref,
                  xbuf, ybuf, w1b, w3b, w2b, gsem, ssem):
    i = pl.program_id(0)
    nt = pl.num_programs(0)
    tm = xbuf.shape[1]
    n_valid = nv_ref[0]
    slot = i % 2

    def start_gather(tile, s):
        def issue(r, carry):
            tok = src_ref[tile * tm + r]
            pltpu.make_async_copy(hn_ref.at[pl.ds(tok, 1)], xbuf.at[s, pl.ds(r, 1)], gsem.at[s]).start()
            return carry
        lax.fori_loop(0, tm, issue, 0, unroll=8)

    def wait_rows(sem, s, buf, n):
        def wait_full():
            pltpu.make_async_copy(hn_ref.at[pl.ds(0, tm)], buf.at[s], sem.at[s]).wait()

        if isinstance(n, int):
            assert n == tm
            wait_full()
            return
        pl.when(n == tm)(wait_full)

        @pl.when(n < tm)
        def _():
            def one(r, carry):
                pltpu.make_async_copy(hn_ref.at[pl.ds(0, 1)], buf.at[s, pl.ds(0, 1)], sem.at[s]).wait()
                return carry
            lax.fori_loop(0, n, one, 0)

    @pl.when(i == 0)
    def _():
        start_gather(0, 0)

    @pl.when(i < n_valid)
    def _():
        @pl.when((i == 0) | (te_ref[i] != te_ref[jnp.maximum(i - 1, 0)]))
        def _():
            w1b[...] = w1_ref[0].astype(BF16)
            w3b[...] = w3_ref[0].astype(BF16)
            w2b[...] = w2_ref[0].astype(BF16)

        @pl.when(i >= 2)
        def _():
            wait_rows(ssem, slot, ybuf, cnt_ref[jnp.maximum(i - 2, 0)])

        wait_rows(gsem, slot, xbuf, tm)

        def scatter_row(tile, s, r):
            row = dst_ref[tile * tm + r]
            pltpu.make_async_copy(ybuf.at[s, pl.ds(r, 1)], y_ref.at[pl.ds(row, 1)], ssem.at[s]).start()

        def tile_ffn(scatter_prev):
            x = xbuf[slot].astype(BF16)
            nxt = jnp.minimum(i + 1, n_valid - 1)
            for r in range(tm):
                tok = src_ref[nxt * tm + r]
                pltpu.make_async_copy(hn_ref.at[pl.ds(tok, 1)], xbuf.at[1 - slot, pl.ds(r, 1)],
                                      gsem.at[1 - slot]).start()
            if scatter_prev:
                for r in range(tm):
                    scatter_row(i - 1, 1 - slot, r)
            a = _dot(x, w1b[...])
            u = _dot(x, w3b[...])
            hid = (_silu(a) * u).astype(BF16)
            ybuf[slot] = _dot(hid, w2b[...])

        prev_full = (i >= 1) & (cnt_ref[jnp.maximum(i - 1, 0)] == tm)
        pl.when(prev_full)(functools.partial(tile_ffn, True))
        pl.when(jnp.logical_not(prev_full))(functools.partial(tile_ffn, False))

        def issue(r, carry):
            scatter_row(i, slot, r)
            return carry

        @pl.when((cnt_ref[i] == tm) & (i == n_valid - 1))
        def _():
            lax.fori_loop(0, tm, issue, 0, unroll=8)

        @pl.when(cnt_ref[i] < tm)
        def _():
            lax.fori_loop(0, cnt_ref[i], issue, 0)

    @pl.when(i == nt - 1)
    def _():
        wait_rows(gsem, n_valid % 2, xbuf, tm)
        wait_rows(ssem, (n_valid - 1) % 2, ybuf, cnt_ref[n_valid - 1])

        @pl.when(n_valid >= 2)
        def _():
            wait_rows(ssem, n_valid % 2, ybuf, cnt_ref[jnp.maximum(n_valid - 2, 0)])


def moe_experts(hn, tile_expert, n_valid, tile_rows, src_tok, dst_row, w1, w3, w2, *, out_rows):
    n_tiles = tile_expert.shape[0]
    d = hn.shape[1]
    tm = MOE_TM
    wmap = lambda i, te, nv, cnt, s, dd: (te[i], 0, 0)
    return pl.pallas_call(
        _experts_body,
        grid_spec=pltpu.PrefetchScalarGridSpec(
            num_scalar_prefetch=5,
            grid=(n_tiles,),
            in_specs=[
                pl.BlockSpec(memory_space=pl.ANY),
                pl.BlockSpec((1,) + w1.shape[1:], wmap),
                pl.BlockSpec((1,) + w3.shape[1:], wmap),
                pl.BlockSpec((1,) + w2.shape[1:], wmap),
            ],
            out_specs=pl.BlockSpec(memory_space=pl.ANY),
            scratch_shapes=[
                pltpu.VMEM((2, tm, d), F32),
                pltpu.VMEM((2, tm, d), F32),
                pltpu.VMEM(w1.shape[1:], BF16),
                pltpu.VMEM(w3.shape[1:], BF16),
                pltpu.VMEM(w2.shape[1:], BF16),
                pltpu.SemaphoreType.DMA((2,)),
                pltpu.SemaphoreType.DMA((2,)),
            ],
        ),
        out_shape=jax.ShapeDtypeStruct((out_rows, d), F32),
        compiler_params=pltpu.CompilerParams(
            dimension_semantics=("arbitrary",), vmem_limit_bytes=VMEM_LIMIT),
        name="moe_experts",
    )(tile_expert, n_valid, tile_rows, src_tok, dst_row, hn, w1, w3, w2)


def _dispatch_plan(eid, n_experts, n_tiles):
    t = eid.shape[0]
    tm = MOE_TM
    e_flat = eid.reshape(-1)
    onehot = (e_flat[:, None] == jnp.arange(n_experts, dtype=I32)[None, :]).astype(I32)
    rank = jnp.sum((jnp.cumsum(onehot, axis=0) - onehot) * onehot, axis=1)
    counts = jnp.sum(onehot, axis=0)
    padded = (counts + tm - 1) // tm * tm
    ends = jnp.cumsum(padded)
    offs = ends - padded
    pos = offs[e_flat] + rank
    n_sorted = n_tiles * tm
    src_a = jnp.zeros((n_sorted,), I32).at[pos].set(jnp.arange(t * TOP_K, dtype=I32))
    src_tok = src_a // TOP_K
    dst_row = (src_a % TOP_K) * t + src_tok
    n_valid = (ends[-1] // tm).astype(I32)
    tile_start = jnp.arange(n_tiles, dtype=I32) * tm
    tile_expert = jnp.sum((tile_start[:, None] >= ends[None, :]).astype(I32), axis=1)
    last_e = jnp.sum((jnp.maximum(ends[-1] - tm, 0) >= ends).astype(I32))
    tile_expert = jnp.minimum(tile_expert, last_e).astype(I32)
    tile_rows = jnp.clip(counts[tile_expert] - (tile_start - offs[tile_expert]), 0, tm)
    tile_rows = jnp.where(tile_start < ends[-1], tile_rows, 0).astype(I32)
    return tile_expert, n_valid.reshape(1), tile_rows, src_tok, dst_row


def moe_layer(addends, scale, scale_lanes, norm_w, w_group, w_expert, w1, w3, w2, *, rows):
    d = w_group.shape[0]
    n_groups = w_group.shape[1]
    n_experts = w_expert.shape[1]
    w_route = jnp.zeros((d, LANES), F32).at[:, :n_groups].set(w_group).at[:, n_groups:n_groups + n_experts].set(w_expert)
    x, hn, idx, wts = moe_router(addends, rows=rows, scale=scale, scale_lanes=scale_lanes, norm_w=norm_w,
                                 w_route=w_route, n_groups=n_groups, per_group=n_experts // n_groups)
    n_tiles = (rows * TOP_K + n_experts * (MOE_TM - 1) + MOE_TM - 1) // MOE_TM
    plan = _dispatch_plan(idx[:, :TOP_K], n_experts, n_tiles)
    y2 = moe_experts(hn, *plan, w1, w3, w2, out_rows=TOP_K * rows)
    return x, wts, y2


def kernel(x_prompt, x_sample, state_ssm, state_conv, cache_k, cache_v, page_table, meta_tokens,
           norm_mix, norm_ffn, norm_final, m_w_in, m_conv_w, m_conv_b, m_dt_bias, m_A_log, m_D,
           m_norm_w, m_w_out, a_w_qkv, a_w_o, a_logit_bias, moe_w_group, moe_w_expert, moe_w1, moe_w3, moe_w2):
    bp, seq, d = x_prompt.shape
    nb, dec, _ = x_sample.shape
    n_meta = meta_tokens.shape[0]
    assert bp == 1 and dec == SUBLANES
    d_inner = m_w_out.shape[0]
    n_heads_ssm = m_A_log.shape[0]
    conv_dim = m_conv_w.shape[1]
    d_state = state_ssm.shape[-1]
    hd_ssm = d_inner // n_heads_ssm
    n_heads = a_logit_bias.shape[0]
    hd = d // n_heads

    p_len = n_meta + seq
    n_pad = (-p_len) % SSD_CHUNK
    tp = n_pad + p_len
    tq = _round_up(tp, ATTN_TQ)
    ns = nb * dec
    row_s = tq
    rows = _round_up(row_s + ns, ROW_TILE)

    x0 = jnp.concatenate([
        jnp.zeros((n_pad, d), F32), meta_tokens.astype(F32), x_prompt[0],
        jnp.zeros((row_s - tp, d), F32), x_sample.reshape(ns, d),
        jnp.zeros((rows - row_s - ns, d), F32)], axis=0)

    w_in = m_w_in.astype(BF16)
    w_z = w_in[:, :d_inner]
    w_xbc = w_in[:, d_inner:d_inner + conv_dim]
    w_dt = jnp.zeros((d, LANES), BF16).at[:, :n_heads_ssm].set(w_in[:, d_inner + conv_dim:])
    z, xbc, dtr = fused_linear([(x0, 0)], rows=rows, norm_w=norm_mix[0], weights=(w_z, w_xbc, w_dt),
                               out_dtypes=((F32,), (F32,), (F32,)), name="in_proj")

    head_of_lane = jnp.arange(d_inner, dtype=I32) // hd_ssm
    e_sel = (jnp.arange(LANES, dtype=I32)[:, None] == head_of_lane[None, :]).astype(BF16)
    pad_h = LANES - n_heads_ssm
    ssd_params = dict(
        n_heads=n_heads_ssm,
        conv_w=m_conv_w.astype(F32), conv_b=m_conv_b.reshape(1, conv_dim).astype(F32),
        dt_bias=jnp.pad(m_dt_bias.astype(F32), (0, pad_h)).reshape(1, LANES),
        a_log=jnp.pad(m_A_log.astype(F32), (0, pad_h)).reshape(1, LANES),
        d_skip=jnp.repeat(m_D.astype(F32), hd_ssm).reshape(1, d_inner),
        norm_w=m_norm_w.reshape(1, d_inner).astype(F32),
        e_sel=e_sel, e_sel_t=e_sel.T)
    g_p, ssm_p = ssd_scan(xbc[None], dtr[None], z[None], jnp.zeros((1, d_inner, d_state), F32),
                          ssd_params, n_rows=tp, n_first=n_pad, name="ssd_prompt")

    L = SSD_CHUNK
    kw = state_conv.shape[1]
    lead = L - dec - kw

    def seq_chunk(a, pre):
        a = a[row_s:row_s + ns].reshape(nb, dec, a.shape[1])
        return jnp.concatenate([jnp.zeros((nb, lead, a.shape[2]), F32), pre, a], axis=1)

    xbc_s = seq_chunk(xbc, state_conv.astype(F32))
    dt_s = seq_chunk(dtr, jnp.zeros((nb, kw, LANES), F32))
    z_s = seq_chunk(z, jnp.zeros((nb, kw, d_inner), F32))
    g_s, ssm_s = ssd_scan(xbc_s, dt_s, z_s, state_ssm.reshape(nb, d_inner, d_state).astype(F32),
                          ssd_params, n_rows=L, n_first=L - dec, name="ssd_sample")

    g = jnp.concatenate([g_p[0], jnp.zeros((row_s - tp, d_inner), BF16), g_s[:, L - dec:].reshape(ns, d_inner),
                         jnp.zeros((rows - row_s - ns, d_inner), BF16)], axis=0)
    (x1,) = fused_linear([(g, 0)], rows=rows, weights=(m_w_out.astype(BF16),), out_dtypes=((F32,),),
                         residual=x0, name="out_proj")

    conv_p = xbc[tp - kw:tp][None]
    conv_s = xbc[row_s:row_s + ns].reshape(nb, dec, conv_dim)[:, dec - kw:]

    x1, wts0, y0 = moe_layer([(x1, 0)], None, (None,), norm_ffn[0], moe_w_group[0], moe_w_expert[0],
                             moe_w1[0], moe_w3[0], moe_w2[0], rows=rows)

    qscale = (hd ** -0.5) * LOG2E
    w_q = (a_w_qkv[:, :d] * qscale).astype(BF16)
    w_k = a_w_qkv[:, d:2 * d].astype(BF16)
    w_v = a_w_qkv[:, 2 * d:].astype(BF16)
    x2, qf, qb, kf, kb, vf, vb = fused_linear(
        [(x1, 0), (y0, 0), (y0, rows)], rows=rows, scale=wts0, scale_lanes=(None, 0, 1), norm_w=norm_mix[1],
        weights=(w_q, w_k, w_v), out_dtypes=((F32, BF16), (F32, BF16), (F32, BF16)), emit_x=True, name="qkv_proj")
    bias2 = a_logit_bias.astype(F32) * LOG2E
    aux_w = LANES - hd
    b_hi = bias2.astype(BF16)
    b_lo = (bias2 - b_hi.astype(F32)).astype(BF16)
    aux_q = jnp.zeros((n_heads, aux_w), BF16).at[:, 0].set(b_hi).at[:, 1].set(b_lo).at[:, 2].set(1.0)
    q_aug = jnp.concatenate([qb.reshape(rows, n_heads, hd),
                             jnp.broadcast_to(aux_q[None], (rows, n_heads, aux_w))], axis=2)
    pad_mask = jnp.where(jnp.arange(rows) < n_pad, MASKED_LOGIT, 0.0).astype(BF16)
    aux_k = jnp.concatenate([jnp.ones((rows, n_heads, 2), BF16),
                             jnp.broadcast_to(pad_mask[:, None, None], (rows, n_heads, 1)),
                             jnp.zeros((rows, n_heads, aux_w - 3), BF16)], axis=2)
    k_aug = jnp.concatenate([kb.reshape(rows, n_heads, hd), aux_k], axis=2)
    o_p = attn_prompt(q_aug.reshape(rows, n_heads * LANES), k_aug.reshape(rows, n_heads * LANES), vb,
                      rows=tq, hd=hd)
    bias_col = jnp.repeat(bias2, dec).reshape(n_heads * dec, 1)
    o_s = attn_sample(qf, kf, vf, row_s, cache_k, cache_v, page_table, bias_col, n_heads=n_heads, hd=hd)
    o = jnp.concatenate([o_p, o_s, jnp.zeros((rows - row_s - ns, d), BF16)], axis=0)
    (x3,) = fused_linear([(o, 0)], rows=rows, weights=(a_w_o.astype(BF16),), out_dtypes=((F32,),),
                         residual=x2, name="attn_out_proj")

    x3, wts1, y1 = moe_layer([(x3, 0)], None, (None,), norm_ffn[1], moe_w_group[1], moe_w_expert[1],
                             moe_w1[1], moe_w3[1], moe_w2[1], rows=rows)

    (yn,) = fused_linear([(x3, 0), (y1, 0), (y1, rows)], rows=rows, scale=wts1, scale_lanes=(None, 0, 1),
                         norm_w=norm_final, emit_norm=True, name="final_norm")

    y_prompt = yn[n_pad + n_meta:tp][None]
    y_sample = yn[row_s:row_s + ns].reshape(nb, dec, d)
    k_p = kf[n_pad:tp].reshape(1, p_len, n_heads, hd)
    v_p = vf[n_pad:tp].reshape(1, p_len, n_heads, hd)
    k_s = kf[row_s:row_s + ns].reshape(nb, dec, n_heads, hd)
    v_s = vf[row_s:row_s + ns].reshape(nb, dec, n_heads, hd)
    return (y_prompt, y_sample, ssm_p.reshape(1, n_heads_ssm, hd_ssm, d_state), conv_p,
            k_p, v_p, ssm_s.reshape(nb, n_heads_ssm, hd_ssm, d_state), conv_s, k_s, v_s)
```

```python
import functools
import math

import jax
import jax.numpy as jnp
from jax import lax
from jax.experimental import pallas as pl
from jax.experimental.pallas import tpu as pltpu

F32 = jnp.float32
BF16 = jnp.bfloat16
I32 = jnp.int32

RMS_EPS = 1e-6
LOG2E = 1.4426950408889634
NEG_BIG = -3.0e38

LANES = 128
SUBLANES = 8
VMEM_LIMIT = 48 * 1024 * 1024

SSD_CHUNK = 128
ROW_TILE = 512
ATTN_TQ = 256
ATTN_TK = 128
ATTN_HEADS_PER_STEP = 4
MOE_TM = 256
TOP_K = 2
PAGES_PER_STEP = 8


def _round_up(x, m):
    return (x + m - 1) // m * m


def _dot(a, b):
    return jnp.dot(a, b, preferred_element_type=F32)


def _dot_nt(a, b):
    return lax.dot_general(a, b, (((1,), (1,)), ((), ())), preferred_element_type=F32)


def _split3(x):
    hi = x.astype(BF16)
    r1 = x - hi.astype(F32)
    mid = r1.astype(BF16)
    lo = (r1 - mid.astype(F32)).astype(BF16)
    return hi, mid, lo


def _dot_exact_rhs(sel, x):
    hi, mid, lo = _split3(x)
    return _dot(sel, hi) + _dot(sel, mid) + _dot(sel, lo)


def _dot_exact_lhs(x, sel):
    hi, mid, lo = _split3(x)
    return _dot(hi, sel) + _dot(mid, sel) + _dot(lo, sel)


def _silu(x):
    return x / (1.0 + jnp.exp(-x))


def _softplus(x):
    return jnp.maximum(x, 0.0) + jnp.log1p(jnp.exp(-jnp.abs(x)))


def _linear_body(*refs, n_add, scale_lanes, has_scale, has_norm, n_w, has_res,
                 emit_x, emit_norm, n_outs, tn):
    it = iter(refs)
    add_refs = [next(it) for _ in range(n_add)]
    sc_ref = next(it) if has_scale else None
    nw_ref = next(it) if has_norm else None
    w_refs = [next(it) for _ in range(n_w)]
    res_ref = next(it) if has_res else None
    xo_ref = next(it) if emit_x else None
    no_ref = next(it) if emit_norm else None
    out_refs = [[next(it) for _ in range(k)] for k in n_outs]
    xb_ref = next(it) if n_w else None

    x = None
    for r, lane in zip(add_refs, scale_lanes):
        v = r[...]
        if lane is not None:
            v = v * sc_ref[:, lane:lane + 1]
        x = v if x is None else x + v
    if emit_x:
        xo_ref[...] = x
    if has_norm:
        ms = jnp.mean(x * x, axis=-1, keepdims=True)
        x = (x * lax.rsqrt(ms + RMS_EPS)) * nw_ref[...]
        if emit_norm:
            no_ref[...] = x
    if not n_w:
        return
    xb_ref[...] = x.astype(BF16)
    for k, w_ref in enumerate(w_refs):
        n_cols = w_ref.shape[1]
        for n0 in range(0, n_cols, tn):
            n1 = min(n0 + tn, n_cols)
            r = _dot(xb_ref[...], w_ref[:, n0:n1])
            if has_res and k == 0:
                r = r + res_ref[:, n0:n1]
            for o_ref in out_refs[k]:
                o_ref[:, n0:n1] = r.astype(o_ref.dtype)


def fused_linear(addends, *, rows, scale=None, scale_lanes=None, norm_w=None, weights=(),
                 out_dtypes=(), residual=None, emit_x=False, emit_norm=False,
                 tm=256, tn=512, name="fused_linear"):
    n_add = len(addends)
    if scale_lanes is None:
        scale_lanes = (None,) * n_add
    d_in = addends[0][0].shape[1]
    assert rows % tm == 0
    in_specs, args = [], []
    for arr, row0 in addends:
        assert row0 % tm == 0
        in_specs.append(pl.BlockSpec((tm, d_in), functools.partial(lambda i, b: (i + b, 0), b=row0 // tm)))
        args.append(arr)
    if scale is not None:
        in_specs.append(pl.BlockSpec((tm, LANES), lambda i: (i, 0)))
        args.append(scale)
    if norm_w is not None:
        in_specs.append(pl.BlockSpec((1, d_in), lambda i: (0, 0)))
        args.append(norm_w.reshape(1, d_in).astype(F32))
    for w in weights:
        in_specs.append(pl.BlockSpec(w.shape, lambda i: (0, 0)))
        args.append(w)
    if residual is not None:
        in_specs.append(pl.BlockSpec((tm, residual.shape[1]), lambda i: (i, 0)))
        args.append(residual)
    out_shapes, out_specs = [], []
    if emit_x:
        out_shapes.append(jax.ShapeDtypeStruct((rows, d_in), F32))
        out_specs.append(pl.BlockSpec((tm, d_in), lambda i: (i, 0)))
    if emit_norm:
        out_shapes.append(jax.ShapeDtypeStruct((rows, d_in), F32))
        out_specs.append(pl.BlockSpec((tm, d_in), lambda i: (i, 0)))
    for w, dts in zip(weights, out_dtypes):
        for dt in dts:
            out_shapes.append(jax.ShapeDtypeStruct((rows, w.shape[1]), dt))
            out_specs.append(pl.BlockSpec((tm, w.shape[1]), lambda i: (i, 0)))
    scratch = [pltpu.VMEM((tm, d_in), BF16)] if weights else []
    body = functools.partial(
        _linear_body, n_add=n_add, scale_lanes=tuple(scale_lanes), has_scale=scale is not None,
        has_norm=norm_w is not None, n_w=len(weights), has_res=residual is not None,
        emit_x=emit_x, emit_norm=emit_norm, n_outs=tuple(len(d) for d in out_dtypes), tn=tn)
    return pl.pallas_call(
        body,
        grid=(rows // tm,),
        in_specs=in_specs,
        out_specs=out_specs,
        out_shape=out_shapes,
        scratch_shapes=scratch,
        compiler_params=pltpu.CompilerParams(
            dimension_semantics=("arbitrary",), vmem_limit_bytes=VMEM_LIMIT),
        name=name,
    )(*args)


def _ssd_body(xbc_ref, dt_ref, z_ref, s0_ref, cw_ref, cb_ref, dtb_ref, alog_ref, dsk_ref,
              nw_ref, e_ref, et_ref, g_ref, sout_ref, ext_ref, s_ref, y_ref, *,
              n_first, d_inner, n_groups, d_state, n_heads):
    c = pl.program_id(1)
    L = SSD_CHUNK
    hd = d_inner // n_heads
    gw = n_groups * d_state
    hpg = n_heads // n_groups

    @pl.when(c == 0)
    def _():
        ext_ref[0:SUBLANES, :] = jnp.zeros((SUBLANES, ext_ref.shape[1]), F32)
        s_ref[...] = s0_ref[0]

    ext_ref[SUBLANES:SUBLANES + L, :] = xbc_ref[0]
    conv = cb_ref[...] + ext_ref[5:5 + L, :] * cw_ref[0:1, :]
    conv = conv + ext_ref[6:6 + L, :] * cw_ref[1:2, :]
    conv = conv + ext_ref[7:7 + L, :] * cw_ref[2:3, :]
    conv = conv + ext_ref[8:8 + L, :] * cw_ref[3:4, :]
    ext_ref[0:SUBLANES, :] = ext_ref[L:L + SUBLANES, :]

    row = lax.broadcasted_iota(I32, (L, 1), 0)
    n0 = jnp.where(c == 0, n_first, 0)
    valid = (row >= n0).astype(F32)
    xbc = _silu(conv) * valid
    xs = xbc[:, :d_inner]

    dt = _softplus(dt_ref[0] + dtb_ref[...]) * valid
    a = dt * (-jnp.exp(alog_ref[...]))
    ri = lax.broadcasted_iota(I32, (L, L), 0)
    ci = lax.broadcasted_iota(I32, (L, L), 1)
    tri = ri >= ci
    acs = _dot_exact_rhs(tri.astype(BF16), a)
    acs_t = acs.T
    acs_last = acs[L - 1:L, :]

    e_sel = e_ref[...]
    dt_e = _dot_exact_lhs(dt, e_sel)
    grow_e = _dot_exact_lhs(jnp.exp(acs), e_sel)
    dec_e = _dot_exact_lhs(jnp.exp(acs_last - acs), e_sel)
    xdt = xs * dt_e
    xw = xdt * dec_e

    lane = lax.broadcasted_iota(I32, (L, LANES), 1)
    lo_half = lane < hd
    s_new = []
    for g in range(n_groups):
        bg = xbc[:, d_inner + g * d_state:d_inner + (g + 1) * d_state].astype(BF16)
        cg = xbc[:, d_inner + gw + g * d_state:d_inner + gw + (g + 1) * d_state].astype(BF16)
        cb = _dot_nt(cg, bg)
        for j in range(hpg // 2):
            h0 = g * hpg + 2 * j
            ms = []
            for h in (h0, h0 + 1):
                diff = acs[:, h:h + 1] - acs_t[h:h + 1, :]
                ms.append((cb * jnp.where(tri, jnp.exp(diff), 0.0)).astype(BF16))
            lhs = jnp.concatenate(ms, axis=1)
            xp = xdt[:, h0 * hd:(h0 + 2) * hd]
            rhs = jnp.concatenate([jnp.where(lo_half, xp, 0.0), jnp.where(lo_half, 0.0, xp)],
                                  axis=0).astype(BF16)
            y_ref[:, h0 * hd:(h0 + 2) * hd] = _dot(lhs, rhs)
        c0, c1 = g * hpg * hd, (g + 1) * hpg * hd
        s_g = s_ref[c0:c1, :]
        y_off = _dot_nt(cg, s_g.astype(BF16)) * grow_e[:, c0:c1]
        y_ref[:, c0:c1] = y_ref[:, c0:c1] + y_off
        s_new.append(_dot(xw[:, c0:c1].T.astype(BF16), bg))

    last_col = jnp.broadcast_to(acs_t[:, L - 1:L], (LANES, LANES))
    cdec = jnp.exp(_dot_exact_rhs(et_ref[...], last_col))
    s_ref[...] = s_ref[...] * cdec + jnp.concatenate(s_new, axis=0)

    y = y_ref[...] + xs * dsk_ref[...]
    gt = y * _silu(z_ref[0])
    gsz = d_inner // n_groups
    outs = []
    for g in range(n_groups):
        gg = gt[:, g * gsz:(g + 1) * gsz]
        ms = jnp.mean(gg * gg, axis=-1, keepdims=True)
        outs.append(gg * lax.rsqrt(ms + RMS_EPS))
    g_ref[0] = (jnp.concatenate(outs, axis=1) * nw_ref[...]).astype(g_ref.dtype)

    @pl.when(c == pl.num_programs(1) - 1)
    def _():
        sout_ref[0] = s_ref[...]


def ssd_scan(xbc, dt, z, s0, params, *, n_rows, n_first, name):
    nb, _, conv_dim = xbc.shape
    t = n_rows
    d_inner = z.shape[2]
    d_state = s0.shape[2]
    n_heads = params["n_heads"]
    n_groups = (conv_dim - d_inner) // (2 * d_state)
    assert t % SSD_CHUNK == 0 and (d_inner // n_heads) * 2 == LANES
    nc = t // SSD_CHUNK
    L = SSD_CHUNK
    const = lambda b, c: (0, 0)
    body = functools.partial(_ssd_body, n_first=n_first, d_inner=d_inner, n_groups=n_groups,
                             d_state=d_state, n_heads=n_heads)
    return pl.pallas_call(
        body,
        grid=(nb, nc),
        in_specs=[
            pl.BlockSpec((1, L, conv_dim), lambda b, c: (b, c, 0)),
            pl.BlockSpec((1, L, LANES), lambda b, c: (b, c, 0)),
            pl.BlockSpec((1, L, d_inner), lambda b, c: (b, c, 0)),
            pl.BlockSpec((1, d_inner, d_state), lambda b, c: (b, 0, 0)),
            pl.BlockSpec(params["conv_w"].shape, const),
            pl.BlockSpec(params["conv_b"].shape, const),
            pl.BlockSpec(params["dt_bias"].shape, const),
            pl.BlockSpec(params["a_log"].shape, const),
            pl.BlockSpec(params["d_skip"].shape, const),
            pl.BlockSpec(params["norm_w"].shape, const),
            pl.BlockSpec(params["e_sel"].shape, const),
            pl.BlockSpec(params["e_sel_t"].shape, const),
        ],
        out_specs=[
            pl.BlockSpec((1, L, d_inner), lambda b, c: (b, c, 0)),
            pl.BlockSpec((1, d_inner, d_state), lambda b, c: (b, 0, 0)),
        ],
        out_shape=[
            jax.ShapeDtypeStruct((nb, t, d_inner), BF16),
            jax.ShapeDtypeStruct((nb, d_inner, d_state), F32),
        ],
        scratch_shapes=[
            pltpu.VMEM((L + SUBLANES, conv_dim), F32),
            pltpu.VMEM((d_inner, d_state), F32),
            pltpu.VMEM((L, d_inner), F32),
        ],
        compiler_params=pltpu.CompilerParams(
            dimension_semantics=("arbitrary", "arbitrary"), vmem_limit_bytes=VMEM_LIMIT),
        name=name,
    )(xbc, dt, z, s0, params["conv_w"], params["conv_b"], params["dt_bias"], params["a_log"],
      params["d_skip"], params["norm_w"], params["e_sel"], params["e_sel_t"])


MASKED_LOGIT = -1e30


def _suffix_sum_matrix(tk):
    kr = lax.broadcasted_iota(I32, (tk, tk), 0)
    kc = lax.broadcasted_iota(I32, (tk, tk), 1)
    return jnp.where(kr >= kc, -1.0, 0.0).astype(BF16)


def _sb_stage_a(z, vis):
    neg_abs = pltpu.bitcast(pltpu.bitcast(z, jnp.uint32) | jnp.uint32(0x80000000), F32)
    sp = jnp.maximum(z, 0.0) + jnp.log2(1.0 + jnp.exp2(neg_abs))
    if vis is not None:
        sp = jnp.where(vis, sp, 0.0)
        z = jnp.where(vis, z, MASKED_LOGIT)
    return z, sp.astype(BF16)


def _sb_stage_b(z, sp_split, carry, u2neg):
    suffix = _dot(sp_split, u2neg)
    return jnp.exp2(z + suffix + carry), carry + suffix[:, 0:1]


def _attn_prompt_body(q_ref, k_ref, v_ref, o_ref, z_scr, sp_scr, *, hd):
    qi = pl.program_id(1)
    tq, tk = ATTN_TQ, ATTN_TK
    nh = ATTN_HEADS_PER_STEP
    u2neg = _suffix_sum_matrix(tk)
    lo_v = lax.broadcasted_iota(I32, (tk, LANES), 1) < hd
    qrow = qi * tq + lax.broadcasted_iota(I32, (tq, tk), 0)
    kidx = lax.broadcasted_iota(I32, (tq, tk), 1)

    def stage_a(j, kb, masked):
        k0 = pl.multiple_of(kb * tk, tk)
        vis = (k0 + kidx) < qrow if masked else None
        for hh in range(nh):
            cols = slice(hh * LANES, (hh + 1) * LANES)
            z, sp = _sb_stage_a(_dot_nt(q_ref[:, cols], k_ref[pl.ds(k0, tk), cols]), vis)
            z_scr[j, hh] = z
            sp_scr[j, hh] = sp

    def stage_b(j, kb, carries, accs):
        k0 = pl.multiple_of(kb * tk, tk)
        carries, accs = list(carries), list(accs)
        for p in range(nh // 2):
            vblk = v_ref[pl.ds(k0, tk), p * LANES:(p + 1) * LANES]
            ws = []
            for hh in (2 * p, 2 * p + 1):
                w, carries[hh] = _sb_stage_b(z_scr[j, hh], sp_scr[j, hh], carries[hh], u2neg)
                ws.append(w.astype(BF16))
            vcat = jnp.concatenate([jnp.where(lo_v, vblk, 0), jnp.where(lo_v, 0, vblk)], axis=0)
            accs[p] = accs[p] + _dot(jnp.concatenate(ws, axis=1), vcat)
        return tuple(carries), tuple(accs)

    nd = tq // tk
    n_full = qi * nd
    carries = tuple(jnp.zeros((tq, 1), F32) for _ in range(nh))
    accs = tuple(jnp.zeros((tq, LANES), F32) for _ in range(nh // 2))
    for j in range(nd):
        stage_a(j, n_full + nd - 1 - j, True)

    def step(t, state):
        carries, accs = state
        kb0 = n_full - 1 - nd * t
        for j in range(nd):
            carries, accs = stage_b(j, kb0 + nd - j, carries, accs)
        for j in range(nd):
            stage_a(j, kb0 - j, False)
        return carries, accs

    carries, accs = lax.fori_loop(0, qi, step, (carries, accs))
    for j in range(nd):
        carries, accs = stage_b(j, nd - 1 - j, carries, accs)
    for p in range(nh // 2):
        o_ref[:, p * LANES:(p + 1) * LANES] = accs[p].astype(o_ref.dtype)


def attn_prompt(q_aug, k_aug, vb, *, rows, hd):
    d = vb.shape[1]
    nh = ATTN_HEADS_PER_STEP
    n_heads = d // hd
    assert ATTN_TQ % ATTN_TK == 0 and n_heads % nh == 0 and nh % 2 == 0 and 2 * hd == LANES
    body = functools.partial(_attn_prompt_body, hd=hd)
    resident = dict(pipeline_mode=pl.Buffered(1))
    return pl.pallas_call(
        body,
        grid=(n_heads // nh, rows // ATTN_TQ),
        in_specs=[
            pl.BlockSpec((ATTN_TQ, nh * LANES), lambda g, i: (i, g)),
            pl.BlockSpec((rows, nh * LANES), lambda g, i: (0, g), **resident),
            pl.BlockSpec((rows, nh * hd), lambda g, i: (0, g), **resident),
        ],
        out_specs=pl.BlockSpec((ATTN_TQ, nh * hd), lambda g, i: (i, g)),
        out_shape=jax.ShapeDtypeStruct((rows, d), BF16),
        scratch_shapes=[
            pltpu.VMEM((ATTN_TQ // ATTN_TK, nh, ATTN_TQ, ATTN_TK), F32),
            pltpu.VMEM((ATTN_TQ // ATTN_TK, nh, ATTN_TQ, ATTN_TK), BF16),
        ],
        compiler_params=pltpu.CompilerParams(
            dimension_semantics=("arbitrary", "arbitrary"), vmem_limit_bytes=VMEM_LIMIT),
        name="attn_prompt",
    )(q_aug, k_aug, vb)


def _attn_sample_body(pt_ref, q_ref, kn_ref, vn_ref, bias_ref, *rest, n_heads, hd, pps, dec):
    k_refs = rest[:pps]
    v_refs = rest[pps:2 * pps]
    o_ref = rest[2 * pps]
    qbd_ref, kpad_ref, vpad_ref, acc_ref, carry_ref = rest[2 * pps + 1:]
    j = pl.program_id(1)
    nr = n_heads * dec
    ps = k_refs[0].shape[2]
    u2neg = _suffix_sum_matrix(ps)
    b2 = bias_ref[...]

    def process(kblk, vblk, vis):
        z = _dot(qbd_ref[...], kblk) + b2
        w, carry = _sb_stage_b(*_sb_stage_a(z, vis), carry_ref[...], u2neg)
        carry_ref[...] = carry
        acc_ref[...] += _dot_nt(w.astype(BF16), vblk)

    @pl.when(j == 0)
    def _():
        d = q_ref.shape[1]
        qt = jnp.concatenate([q_ref[...]] * n_heads, axis=0)
        rh = lax.broadcasted_iota(I32, (nr, d), 0) // dec
        lh = lax.broadcasted_iota(I32, (nr, d), 1) // hd
        qbd_ref[...] = jnp.where(rh == lh, qt, 0.0).astype(BF16)
        kpad_ref[...] = jnp.zeros(kpad_ref.shape, F32)
        vpad_ref[...] = jnp.zeros(vpad_ref.shape, F32)
        kpad_ref[0:dec, :] = kn_ref[...]
        vpad_ref[0:dec, :] = vn_ref[...]
        acc_ref[...] = jnp.zeros(acc_ref.shape, F32)
        carry_ref[...] = jnp.zeros(carry_ref.shape, F32)
        qrow = lax.broadcasted_iota(I32, (nr, ps), 0) % dec
        kidx = lax.broadcasted_iota(I32, (nr, ps), 1)
        process(kpad_ref[...].T.astype(BF16), vpad_ref[...].T.astype(BF16), kidx < qrow)

    @pl.when(j > 0)
    def _():
        order = range(pps - 1, -1, -1)
        zs = [_dot(qbd_ref[...], k_refs[i][0].astype(BF16)) + b2 for i in order]
        carry = carry_ref[...]
        ws = []
        for z in zs:
            w, carry = _sb_stage_b(*_sb_stage_a(z, None), carry, u2neg)
            ws.append(w.astype(BF16))
        carry_ref[...] = carry
        vcat = jnp.concatenate([v_refs[i][0].astype(BF16) for i in order], axis=1)
        acc_ref[...] += _dot_nt(jnp.concatenate(ws, axis=1), vcat)

    @pl.when(j == pl.num_programs(1) - 1)
    def _():
        d = o_ref.shape[1]
        lh = lax.broadcasted_iota(I32, (dec, d), 1) // hd
        out = jnp.zeros((dec, d), F32)
        for h in range(n_heads):
            out = out + jnp.where(lh == h, acc_ref[h * dec:(h + 1) * dec, :], 0.0)
        o_ref[...] = out.astype(o_ref.dtype)


def attn_sample(q, k, v, row0, cache_k, cache_v, page_table, bias_col, *, n_heads, hd):
    nb, n_pages = page_table.shape
    n_pool, ps = cache_k.shape[0], cache_k.shape[1]
    d = n_heads * hd
    dec = SUBLANES
    nr = n_heads * dec
    assert nr == ps == LANES and row0 % dec == 0
    pps = min(PAGES_PER_STEP, n_pages)
    assert n_pages % pps == 0
    n_steps = n_pages // pps
    ck = jnp.transpose(cache_k, (0, 2, 3, 1)).reshape(n_pool, d, ps)
    cv = jnp.transpose(cache_v, (0, 2, 3, 1)).reshape(n_pool, d, ps)
    pt = page_table.reshape(-1).astype(I32)
    rb0 = row0 // dec

    def page_map(b, j, pt_ref, *, i):
        jj = jnp.maximum(j, 1)
        return (pt_ref[b * n_pages + n_pages - jj * pps + i], 0, 0)

    new_map = lambda b, j, pt_ref: (rb0 + b, 0)
    in_specs = [
        pl.BlockSpec((dec, d), new_map),
        pl.BlockSpec((dec, d), new_map),
        pl.BlockSpec((dec, d), new_map),
        pl.BlockSpec((nr, 1), lambda b, j, pt_ref: (0, 0)),
    ]
    in_specs += [pl.BlockSpec((1, d, ps), functools.partial(page_map, i=i)) for i in range(pps)]
    in_specs += [pl.BlockSpec((1, d, ps), functools.partial(page_map, i=i)) for i in range(pps)]
    body = functools.partial(_attn_sample_body, n_heads=n_heads, hd=hd, pps=pps, dec=dec)
    return pl.pallas_call(
        body,
        grid_spec=pltpu.PrefetchScalarGridSpec(
            num_scalar_prefetch=1,
            grid=(nb, n_steps + 1),
            in_specs=in_specs,
            out_specs=pl.BlockSpec((dec, d), lambda b, j, pt_ref: (b, 0)),
            scratch_shapes=[
                pltpu.VMEM((nr, d), BF16),
                pltpu.VMEM((ps, d), F32),
                pltpu.VMEM((ps, d), F32),
                pltpu.VMEM((nr, d), F32),
                pltpu.VMEM((nr, 1), F32),
            ],
        ),
        out_shape=jax.ShapeDtypeStruct((nb * dec, d), BF16),
        compiler_params=pltpu.CompilerParams(
            dimension_semantics=("arbitrary", "arbitrary"), vmem_limit_bytes=VMEM_LIMIT),
        name="attn_sample",
    )(pt, q, k, v, bias_col, *([ck] * pps), *([cv] * pps))


def _router_body(*refs, n_add, scale_lanes, n_groups, per_group):
    it = iter(refs)
    add_refs = [next(it) for _ in range(n_add)]
    sc_ref = next(it) if any(l is not None for l in scale_lanes) else None
    nw_ref = next(it)
    wr_ref = next(it)
    hn_ref, idx_ref, wt_ref = next(it), next(it), next(it)

    x = None
    for r, ln in zip(add_refs, scale_lanes):
        v = r[...]
        if ln is not None:
            v = v * sc_ref[:, ln:ln + 1]
        x = v if x is None else x + v
    ms = jnp.mean(x * x, axis=-1, keepdims=True)
    hn = (x * lax.rsqrt(ms + RMS_EPS)) * nw_ref[...]
    hn_ref[...] = hn
    logits = jnp.dot(hn, wr_ref[...], preferred_element_type=F32, precision=lax.Precision.HIGHEST)

    tm = logits.shape[0]
    lane = lax.broadcasted_iota(I32, (tm, LANES), 1)
    far = jnp.int32(4 * LANES)
    gl = jnp.where(lane < n_groups, logits, NEG_BIG)
    gmax = jnp.max(gl, axis=1, keepdims=True)
    gidx = jnp.min(jnp.where(gl == gmax, lane, far), axis=1, keepdims=True)
    gsum = jnp.sum(jnp.where(lane < n_groups, jnp.exp(logits - gmax), 0.0), axis=1, keepdims=True)
    g_w = 1.0 / gsum
    lo = n_groups + gidx * per_group
    el = jnp.where(lane >= lo, jnp.where(lane < lo + per_group, logits, NEG_BIG), NEG_BIG)
    v1 = jnp.max(el, axis=1, keepdims=True)
    i1 = jnp.min(jnp.where(el == v1, lane, far), axis=1, keepdims=True)
    el2 = jnp.where(lane == i1, NEG_BIG, el)
    v2 = jnp.max(el2, axis=1, keepdims=True)
    i2 = jnp.min(jnp.where(el2 == v2, lane, far), axis=1, keepdims=True)
    e2 = jnp.exp(v2 - v1)
    den = 1.0 + e2
    w1 = (1.0 / den) * g_w
    w2 = (e2 / den) * g_w
    idx_ref[...] = jnp.where(lane == 0, i1 - n_groups, jnp.where(lane == 1, i2 - n_groups, 0))
    wt_ref[...] = jnp.where(lane == 0, w1, jnp.where(lane == 1, w2, 0.0))


def moe_router(addends, *, rows, scale, scale_lanes, norm_w, w_route, n_groups, per_group, tm=256):
    n_add = len(addends)
    d = addends[0][0].shape[1]
    in_specs, args = [], []
    for arr, row0 in addends:
        in_specs.append(pl.BlockSpec((tm, d), functools.partial(lambda i, b: (i + b, 0), b=row0 // tm)))
        args.append(arr)
    if scale is not None:
        in_specs.append(pl.BlockSpec((tm, LANES), lambda i: (i, 0)))
        args.append(scale)
    in_specs.append(pl.BlockSpec((1, d), lambda i: (0, 0)))
    args.append(norm_w.reshape(1, d).astype(F32))
    in_specs.append(pl.BlockSpec(w_route.shape, lambda i: (0, 0)))
    args.append(w_route)
    row_spec = pl.BlockSpec((tm, d), lambda i: (i, 0))
    lane_spec = pl.BlockSpec((tm, LANES), lambda i: (i, 0))
    body = functools.partial(_router_body, n_add=n_add, scale_lanes=tuple(scale_lanes),
                             n_groups=n_groups, per_group=per_group)
    return pl.pallas_call(
        body,
        grid=(rows // tm,),
        in_specs=in_specs,
        out_specs=[row_spec, lane_spec, lane_spec],
        out_shape=[
            jax.ShapeDtypeStruct((rows, d), F32),
            jax.ShapeDtypeStruct((rows, LANES), I32),
            jax.ShapeDtypeStruct((rows, LANES), F32),
        ],
        compiler_params=pltpu.CompilerParams(
            dimension_semantics=("arbitrary",), vmem_limit_bytes=VMEM_LIMIT),
        name="moe_router",
    )(*args)


def _experts_body(te_ref, nv_ref, cnt_ref, src_ref, dst_ref, hn_ref, w1_ref, w3_ref, w2_ref, y_ref,
                  xbuf, ybuf, w1b, w3b, w2b, gsem, ssem):
    i = pl.program_id(0)
    nt = pl.num_programs(0)
    tm = xbuf.shape[1]
    n_valid = nv_ref[0]
    slot = i % 2

    def start_gather(tile, s):
        def issue(r, carry):
            tok = src_ref[tile * tm + r]
            pltpu.make_async_copy(hn_ref.at[pl.ds(tok, 1)], xbuf.at[s, pl.ds(r, 1)], gsem.at[s]).start()
            return carry
        lax.fori_loop(0, tm, issue, 0, unroll=8)

    def wait_rows(sem, s, buf, n):
        def wait_full():
            pltpu.make_async_copy(hn_ref.at[pl.ds(0, tm)], buf.at[s], sem.at[s]).wait()

        if isinstance(n, int):
            assert n == tm
            wait_full()
            return
        pl.when(n == tm)(wait_full)

        @pl.when(n < tm)
        def _():
            def one(r, carry):
                pltpu.make_async_copy(hn_ref.at[pl.ds(0, 1)], buf.at[s, pl.ds(0, 1)], sem.at[s]).wait()
                return carry
            lax.fori_loop(0, n, one, 0)

    @pl.when(i == 0)
    def _():
        start_gather(0, 0)

    @pl.when(i < n_valid)
    def _():
        @pl.when((i == 0) | (te_ref[i] != te_ref[jnp.maximum(i - 1, 0)]))
        def _():
            w1b[...] = w1_ref[0].astype(BF16)
            w3b[...] = w3_ref[0].astype(BF16)
            w2b[...] = w2_ref[0].astype(BF16)

        @pl.when(i >= 2)
        def _():
            wait_rows(ssem, slot, ybuf, cnt_ref[jnp.maximum(i - 2, 0)])

        wait_rows(gsem, slot, xbuf, tm)

        def scatter_row(tile, s, r):
            row = dst_ref[tile * tm + r]
            pltpu.make_async_copy(ybuf.at[s, pl.ds(r, 1)], y_ref.at[pl.ds(row, 1)], ssem.at[s]).start()

        def tile_ffn(scatter_prev):
            x = xbuf[slot].astype(BF16)
            nxt = jnp.minimum(i + 1, n_valid - 1)
            for r in range(tm):
                tok = src_ref[nxt * tm + r]
                pltpu.make_async_copy(hn_ref.at[pl.ds(tok, 1)], xbuf.at[1 - slot, pl.ds(r, 1)],
                                      gsem.at[1 - slot]).start()
            if scatter_prev:
                for r in range(tm):
                    scatter_row(i - 1, 1 - slot, r)
            a = _dot(x, w1b[...])
            u = _dot(x, w3b[...])
            hid = (_silu(a) * u).astype(BF16)
            ybuf[slot] = _dot(hid, w2b[...])

        prev_full = (i >= 1) & (cnt_ref[jnp.maximum(i - 1, 0)] == tm)
        pl.when(prev_full)(functools.partial(tile_ffn, True))
        pl.when(jnp.logical_not(prev_full))(functools.partial(tile_ffn, False))

        def issue(r, carry):
            scatter_row(i, slot, r)
            return carry

        @pl.when((cnt_ref[i] == tm) & (i == n_valid - 1))
        def _():
            lax.fori_loop(0, tm, issue, 0, unroll=8)

        @pl.when(cnt_ref[i] < tm)
        def _():
            lax.fori_loop(0, cnt_ref[i], issue, 0)

    @pl.when(i == nt - 1)
    def _():
        wait_rows(gsem, n_valid % 2, xbuf, tm)
        wait_rows(ssem, (n_valid - 1) % 2, ybuf, cnt_ref[n_valid - 1])

        @pl.when(n_valid >= 2)
        def _():
            wait_rows(ssem, n_valid % 2, ybuf, cnt_ref[jnp.maximum(n_valid - 2, 0)])


def moe_experts(hn, tile_expert, n_valid, tile_rows, src_tok, dst_row, w1, w3, w2, *, out_rows):
    n_tiles = tile_expert.shape[0]
    d = hn.shape[1]
    tm = MOE_TM
    wmap = lambda i, te, nv, cnt, s, dd: (te[i], 0, 0)
    return pl.pallas_call(
        _experts_body,
        grid_spec=pltpu.PrefetchScalarGridSpec(
            num_scalar_prefetch=5,
            grid=(n_tiles,),
            in_specs=[
                pl.BlockSpec(memory_space=pl.ANY),
                pl.BlockSpec((1,) + w1.shape[1:], wmap),
                pl.BlockSpec((1,) + w3.shape[1:], wmap),
                pl.BlockSpec((1,) + w2.shape[1:], wmap),
            ],
            out_specs=pl.BlockSpec(memory_space=pl.ANY),
            scratch_shapes=[
                pltpu.VMEM((2, tm, d), F32),
                pltpu.VMEM((2, tm, d), F32),
                pltpu.VMEM(w1.shape[1:], BF16),
                pltpu.VMEM(w3.shape[1:], BF16),
                pltpu.VMEM(w2.shape[1:], BF16),
                pltpu.SemaphoreType.DMA((2,)),
                pltpu.SemaphoreType.DMA((2,)),
            ],
        ),
        out_shape=jax.ShapeDtypeStruct((out_rows, d), F32),
        compiler_params=pltpu.CompilerParams(
            dimension_semantics=("arbitrary",), vmem_limit_bytes=VMEM_LIMIT),
        name="moe_experts",
    )(tile_expert, n_valid, tile_rows, src_tok, dst_row, hn, w1, w3, w2)


def _dispatch_plan(eid, n_experts, n_tiles):
    t = eid.shape[0]
    tm = MOE_TM
    e_flat = eid.reshape(-1)
    onehot = (e_flat[:, None] == jnp.arange(n_experts, dtype=I32)[None, :]).astype(I32)
    rank = jnp.sum((jnp.cumsum(onehot, axis=0) - onehot) * onehot, axis=1)
    counts = jnp.sum(onehot, axis=0)
    padded = (counts + tm - 1) // tm * tm
    ends = jnp.cumsum(padded)
    offs = ends - padded
    pos = offs[e_flat] + rank
    n_sorted = n_tiles * tm
    src_a = jnp.zeros((n_sorted,), I32).at[pos].set(jnp.arange(t * TOP_K, dtype=I32))
    src_tok = src_a // TOP_K
    dst_row = (src_a % TOP_K) * t + src_tok
    n_valid = (ends[-1] // tm).astype(I32)
    tile_start = jnp.arange(n_tiles, dtype=I32) * tm
    tile_expert = jnp.sum((tile_start[:, None] >= ends[None, :]).astype(I32), axis=1)
    last_e = jnp.sum((jnp.maximum(ends[-1] - tm, 0) >= ends).astype(I32))
    tile_expert = jnp.minimum(tile_expert, last_e).astype(I32)
    tile_rows = jnp.clip(counts[tile_expert] - (tile_start - offs[tile_expert]), 0, tm)
    tile_rows = jnp.where(tile_start < ends[-1], tile_rows, 0).astype(I32)
    return tile_expert, n_valid.reshape(1), tile_rows, src_tok, dst_row


def moe_layer(addends, scale, scale_lanes, norm_w, w_group, w_expert, w1, w3, w2, *, rows):
    d = w_group.shape[0]
    n_groups = w_group.shape[1]
    n_experts = w_expert.shape[1]
    w_route = jnp.zeros((d, LANES), F32).at[:, :n_groups].set(w_group).at[:, n_groups:n_groups + n_experts].set(w_expert)
    hn, idx, wts = moe_router(addends, rows=rows, scale=scale, scale_lanes=scale_lanes, norm_w=norm_w,
                                 w_route=w_route, n_groups=n_groups, per_group=n_experts // n_groups)
    n_tiles = (rows * TOP_K + n_experts * (MOE_TM - 1) + MOE_TM - 1) // MOE_TM
    plan = _dispatch_plan(idx[:, :TOP_K], n_experts, n_tiles)
    y2 = moe_experts(hn, *plan, w1, w3, w2, out_rows=TOP_K * rows)
    return wts, y2


def kernel(x_prompt, x_sample, state_ssm, state_conv, cache_k, cache_v, page_table, meta_tokens,
           norm_mix, norm_ffn, norm_final, m_w_in, m_conv_w, m_conv_b, m_dt_bias, m_A_log, m_D,
           m_norm_w, m_w_out, a_w_qkv, a_w_o, a_logit_bias, moe_w_group, moe_w_expert, moe_w1, moe_w3, moe_w2):
    bp, seq, d = x_prompt.shape
    nb, dec, _ = x_sample.shape
    n_meta = meta_tokens.shape[0]
    assert bp == 1 and dec == SUBLANES
    d_inner = m_w_out.shape[0]
    n_heads_ssm = m_A_log.shape[0]
    conv_dim = m_conv_w.shape[1]
    d_state = state_ssm.shape[-1]
    hd_ssm = d_inner // n_heads_ssm
    n_heads = a_logit_bias.shape[0]
    hd = d // n_heads

    p_len = n_meta + seq
    n_pad = (-p_len) % SSD_CHUNK
    tp = n_pad + p_len
    tq = _round_up(tp, ATTN_TQ)
    ns = nb * dec
    row_s = tq
    rows = _round_up(row_s + ns, ROW_TILE)

    x0 = jnp.concatenate([
        jnp.zeros((n_pad, d), F32), meta_tokens.astype(F32), x_prompt[0],
        jnp.zeros((row_s - tp, d), F32), x_sample.reshape(ns, d),
        jnp.zeros((rows - row_s - ns, d), F32)], axis=0)

    w_in = m_w_in.astype(BF16)
    w_z = w_in[:, :d_inner]
    w_xbc = w_in[:, d_inner:d_inner + conv_dim]
    w_dt = jnp.zeros((d, LANES), BF16).at[:, :n_heads_ssm].set(w_in[:, d_inner + conv_dim:])
    z, xbc, dtr = fused_linear([(x0, 0)], rows=rows, norm_w=norm_mix[0], weights=(w_z, w_xbc, w_dt),
                               out_dtypes=((F32,), (F32,), (F32,)), name="in_proj")

    head_of_lane = jnp.arange(d_inner, dtype=I32) // hd_ssm
    e_sel = (jnp.arange(LANES, dtype=I32)[:, None] == head_of_lane[None, :]).astype(BF16)
    pad_h = LANES - n_heads_ssm
    ssd_params = dict(
        n_heads=n_heads_ssm,
        conv_w=m_conv_w.astype(F32), conv_b=m_conv_b.reshape(1, conv_dim).astype(F32),
        dt_bias=jnp.pad(m_dt_bias.astype(F32), (0, pad_h)).reshape(1, LANES),
        a_log=jnp.pad(m_A_log.astype(F32), (0, pad_h)).reshape(1, LANES),
        d_skip=jnp.repeat(m_D.astype(F32), hd_ssm).reshape(1, d_inner),
        norm_w=m_norm_w.reshape(1, d_inner).astype(F32),
        e_sel=e_sel, e_sel_t=e_sel.T)
    g_p, ssm_p = ssd_scan(xbc[None], dtr[None], z[None], jnp.zeros((1, d_inner, d_state), F32),
                          ssd_params, n_rows=tp, n_first=n_pad, name="ssd_prompt")

    L = SSD_CHUNK
    kw = state_conv.shape[1]
    lead = L - dec - kw

    def seq_chunk(a, pre):
        a = a[row_s:row_s + ns].reshape(nb, dec, a.shape[1])
        return jnp.concatenate([jnp.zeros((nb, lead, a.shape[2]), F32), pre, a], axis=1)

    xbc_s = seq_chunk(xbc, state_conv.astype(F32))
    dt_s = seq_chunk(dtr, jnp.zeros((nb, kw, LANES), F32))
    z_s = seq_chunk(z, jnp.zeros((nb, kw, d_inner), F32))
    g_s, ssm_s = ssd_scan(xbc_s, dt_s, z_s, state_ssm.reshape(nb, d_inner, d_state).astype(F32),
                          ssd_params, n_rows=L, n_first=L - dec, name="ssd_sample")

    g = jnp.concatenate([g_p[0], jnp.zeros((row_s - tp, d_inner), BF16), g_s[:, L - dec:].reshape(ns, d_inner),
                         jnp.zeros((rows - row_s - ns, d_inner), BF16)], axis=0)
    (x1,) = fused_linear([(g, 0)], rows=rows, weights=(m_w_out.astype(BF16),), out_dtypes=((F32,),),
                         residual=x0, name="out_proj")

    conv_p = xbc[tp - kw:tp][None]
    conv_s = xbc[row_s:row_s + ns].reshape(nb, dec, conv_dim)[:, dec - kw:]

    wts0, y0 = moe_layer([(x1, 0)], None, (None,), norm_ffn[0], moe_w_group[0], moe_w_expert[0],
                         moe_w1[0], moe_w3[0], moe_w2[0], rows=rows)

    qscale = (hd ** -0.5) * LOG2E
    w_q = (a_w_qkv[:, :d] * qscale).astype(BF16)
    w_k = a_w_qkv[:, d:2 * d].astype(BF16)
    w_v = a_w_qkv[:, 2 * d:].astype(BF16)
    x2, qf, qb, kf, kb, vf, vb = fused_linear(
        [(x1, 0), (y0, 0), (y0, rows)], rows=rows, scale=wts0, scale_lanes=(None, 0, 1), norm_w=norm_mix[1],
        weights=(w_q, w_k, w_v), out_dtypes=((F32, BF16), (F32, BF16), (F32, BF16)), emit_x=True, name="qkv_proj")
    bias2 = a_logit_bias.astype(F32) * LOG2E
    aux_w = LANES - hd
    b_hi = bias2.astype(BF16)
    b_lo = (bias2 - b_hi.astype(F32)).astype(BF16)
    aux_q = jnp.zeros((n_heads, aux_w), BF16).at[:, 0].set(b_hi).at[:, 1].set(b_lo).at[:, 2].set(1.0)
    q_aug = jnp.concatenate([qb.reshape(rows, n_heads, hd),
                             jnp.broadcast_to(aux_q[None], (rows, n_heads, aux_w))], axis=2)
    pad_mask = jnp.where(jnp.arange(rows) < n_pad, MASKED_LOGIT, 0.0).astype(BF16)
    aux_k = jnp.concatenate([jnp.ones((rows, n_heads, 2), BF16),
                             jnp.broadcast_to(pad_mask[:, None, None], (rows, n_heads, 1)),
                             jnp.zeros((rows, n_heads, aux_w - 3), BF16)], axis=2)
    k_aug = jnp.concatenate([kb.reshape(rows, n_heads, hd), aux_k], axis=2)
    o_p = attn_prompt(q_aug.reshape(rows, n_heads * LANES), k_aug.reshape(rows, n_heads * LANES), vb,
                      rows=tq, hd=hd)
    bias_col = jnp.repeat(bias2, dec).reshape(n_heads * dec, 1)
    o_s = attn_sample(qf, kf, vf, row_s, cache_k, cache_v, page_table, bias_col, n_heads=n_heads, hd=hd)
    o = jnp.concatenate([o_p, o_s, jnp.zeros((rows - row_s - ns, d), BF16)], axis=0)
    (x3,) = fused_linear([(o, 0)], rows=rows, weights=(a_w_o.astype(BF16),), out_dtypes=((F32,),),
                         residual=x2, name="attn_out_proj")

    wts1, y1 = moe_layer([(x3, 0)], None, (None,), norm_ffn[1], moe_w_group[1], moe_w_expert[1],
                         moe_w1[1], moe_w3[1], moe_w2[1], rows=rows)

    (yn,) = fused_linear([(x3, 0), (y1, 0), (y1, rows)], rows=rows, scale=wts1, scale_lanes=(None, 0, 1),
                         norm_w=norm_final, emit_norm=True, name="final_norm")

    y_prompt = yn[n_pad + n_meta:tp][None]
    y_sample = yn[row_s:row_s + ns].reshape(nb, dec, d)
    k_p = kf[n_pad:tp].reshape(1, p_len, n_heads, hd)
    v_p = vf[n_pad:tp].reshape(1, p_len, n_heads, hd)
    k_s = kf[row_s:row_s + ns].reshape(nb, dec, n_heads, hd)
    v_s = vf[row_s:row_s + ns].reshape(nb, dec, n_heads, hd)
    return (y_prompt, y_sample, ssm_p.reshape(1, n_heads_ssm, hd_ssm, d_state), conv_p,
            k_p, v_p, ssm_s.reshape(nb, n_heads_ssm, hd_ssm, d_state), conv_s, k_s, v_s)
```
